```python
import jax, jax.numpy as jnp
from jax import lax
import numpy as np

D_MODEL = 1024
BATCH = 8
SEQ = 4096
DEPTH = 4

GRID_W = 64
CTX_LEN = 256
EPS = 1e-6
ROPE_BASE = 10000.0
CHUNK = 64

GLA_HEADS = 4
GLA_DK = D_MODEL // 16
GLA_DV = D_MODEL // 8
GLA_RANK = 16
GLA_TAU = 16.0

MLSTM_HEADS = 4
MLSTM_DH = D_MODEL // 8

MIX_WIDTH = GLA_HEADS * GLA_DV + MLSTM_HEADS * MLSTM_DH
EVEN_SIZES = (GLA_HEADS * GLA_DK, GLA_HEADS * GLA_DK, GLA_HEADS * GLA_DV, GLA_HEADS * GLA_DV, 2 * GLA_RANK,
              MLSTM_HEADS * MLSTM_DH, MLSTM_HEADS * MLSTM_DH, MLSTM_HEADS * MLSTM_DH, MLSTM_HEADS * MLSTM_DH,
              4 * MLSTM_HEADS)
EVEN_IN = sum(EVEN_SIZES)

NA_HEADS = 16
NA_DH = D_MODEL // NA_HEADS
NA_WIN_H = 8
NA_WIN_W = 16

D_FF = ((8 * D_MODEL // 3 + 255) // 256) * 256
FFN_CONV = 3

kernel_name = 'hybrid_gla_mlstm_natten_convffn_dit'


def rms_norm(x, w):
    xf = x.astype(jnp.float32)
    y = xf * lax.rsqrt(jnp.mean(xf * xf, axis=-1, keepdims=True) + EPS)
    return (y * w.astype(jnp.float32)).astype(x.dtype)


def head_rms_norm(h, w):
    return h * lax.rsqrt(jnp.mean(h * h, axis=-1, keepdims=True) + EPS) * w.astype(jnp.float32)


def axial_rope(n_tokens, head_dim):
    pos = jnp.arange(n_tokens, dtype=jnp.int32)
    row = (pos // GRID_W).astype(jnp.float32)
    col = (pos % GRID_W).astype(jnp.float32)
    n_freq = head_dim // 4
    inv_freq = ROPE_BASE ** (-jnp.arange(n_freq, dtype=jnp.float32) / n_freq)
    ang = jnp.concatenate([row[:, None] * inv_freq, col[:, None] * inv_freq], axis=-1)
    return jnp.cos(ang), jnp.sin(ang)


def apply_rope(x, cos, sin):
    half = x.shape[-1] // 2
    x1, x2 = x[..., :half], x[..., half:]
    c = cos[None, :, None, :]
    s = sin[None, :, None, :]
    return jnp.concatenate([x1 * c - x2 * s, x1 * s + x2 * c], axis=-1)


def gla_chunked(q, k, v, log_a, state0):
    B, H, T, dk = q.shape
    dv = v.shape[-1]
    nc = T // CHUNK
    q = q.reshape(B, H, nc, CHUNK, dk)
    k = k.reshape(B, H, nc, CHUNK, dk)
    v = v.reshape(B, H, nc, CHUNK, dv)
    b = jnp.cumsum(log_a.reshape(B, H, nc, CHUNK, dk), axis=3)
    g = b[:, :, :, -1]
    q_dec = q * jnp.exp(b)
    k_dec = k * jnp.exp(-b)
    k_end = k * jnp.exp(g[:, :, :, None, :] - b)
    causal = jnp.tril(jnp.ones((CHUNK, CHUNK), dtype=bool))
    scores = jnp.where(causal, jnp.einsum('bhnid,bhnjd->bhnij', q_dec, k_dec), 0.0)
    o_intra = jnp.einsum('bhnij,bhnjv->bhniv', scores, v)
    kv = jnp.einsum('bhnjd,bhnjv->bhndv', k_end, v)

    def step(s, inp):
        g_n, kv_n = inp
        return jnp.exp(g_n)[..., None] * s + kv_n, s

    s_final, s_start = lax.scan(step, state0, (jnp.moveaxis(g, 2, 0), jnp.moveaxis(kv, 2, 0)))
    o_inter = jnp.einsum('bhnid,nbhdv->bhniv', q_dec, s_start)
    return (o_intra + o_inter).reshape(B, H, T, dv), s_final


def mlstm_chunked(q, k, v, log_i, log_f, state0):
    B, H, T, dh = q.shape
    nc = T // CHUNK
    q = q.reshape(B, H, nc, CHUNK, dh)
    k = k.reshape(B, H, nc, CHUNK, dh)
    v = v.reshape(B, H, nc, CHUNK, dh)
    log_i = log_i.reshape(B, H, nc, CHUNK)
    b = jnp.cumsum(log_f.reshape(B, H, nc, CHUNK), axis=-1)
    g = b[..., -1]
    causal = jnp.tril(jnp.ones((CHUNK, CHUNK), dtype=bool))
    log_d = jnp.where(causal, b[..., :, None] - b[..., None, :] + log_i[..., None, :], -jnp.inf)
    log_w_end = g[..., None] - b + log_i
    m_loc = jnp.max(log_w_end, axis=-1)
    w_end = jnp.exp(log_w_end - m_loc[..., None])
    c_loc = jnp.einsum('bhns,bhnsd,bhnse->bhnde', w_end, k, v)
    n_loc = jnp.einsum('bhns,bhnsd->bhnd', w_end, k)

    def step(carry, inp):
        c_st, n_st, m_st = carry
        g_n, m_loc_n, c_loc_n, n_loc_n = inp
        m_new = jnp.maximum(g_n + m_st, m_loc_n)
        a = jnp.exp(g_n + m_st - m_new)
        w = jnp.exp(m_loc_n - m_new)
        c_new = a[..., None, None] * c_st + w[..., None, None] * c_loc_n
        n_new = a[..., None] * n_st + w[..., None] * n_loc_n
        return (c_new, n_new, m_new), (c_st, n_st, m_st)

    mv = lambda t: jnp.moveaxis(t, 2, 0)
    final, (c_s, n_s, m_s) = lax.scan(step, state0, (mv(g), mv(m_loc), mv(c_loc), mv(n_loc)))
    log_inter = b + jnp.moveaxis(m_s, 0, 2)[..., None]
    m_t = jnp.maximum(log_inter, jnp.max(log_d, axis=-1))
    d_w = jnp.exp(log_d - m_t[..., None])
    inter_w = jnp.exp(log_inter - m_t)
    s = jnp.einsum('bhntd,bhnsd->bhnts', q, k) * d_w
    numer = jnp.einsum('bhnts,bhnse->bhnte', s, v) + inter_w[..., None] * jnp.einsum('bhntd,nbhde->bhnte', q, c_s)
    denom = jnp.sum(s, axis=-1) + inter_w * jnp.einsum('bhntd,nbhd->bhnt', q, n_s)
    h = numer / jnp.maximum(jnp.abs(denom), jnp.exp(-m_t))[..., None]
    return h.reshape(B, H, T, dh), final


def even_project(u, w_in, a_up, a_bias, ml_gate_bias, rope_gla, rope_ml):
    B, T, _ = u.shape
    z = (u @ w_in).astype(jnp.float32)
    splits = np.cumsum(EVEN_SIZES)[:-1].tolist()
    gq, gk, gv, gg, ga, mq, mk, mv, mo, mg = jnp.split(z, splits, axis=-1)
    gq = gq.reshape(B, T, GLA_HEADS, GLA_DK)
    gk = gk.reshape(B, T, GLA_HEADS, GLA_DK)
    mq = mq.reshape(B, T, MLSTM_HEADS, MLSTM_DH)
    mk = mk.reshape(B, T, MLSTM_HEADS, MLSTM_DH)
    if rope_gla is not None:
        gq, gk = apply_rope(gq, *rope_gla), apply_rope(gk, *rope_gla)
        mq, mk = apply_rope(mq, *rope_ml), apply_rope(mk, *rope_ml)
    bhtd = lambda t: t.transpose(0, 2, 1, 3)
    a_pre = jnp.einsum('btzr,zre->zbte', ga.reshape(B, T, 2, GLA_RANK), a_up) + a_bias[:, None, None, :]
    log_a = (jax.nn.log_sigmoid(a_pre) / GLA_TAU).reshape(2, B, T, GLA_HEADS, GLA_DK).transpose(0, 1, 3, 2, 4)
    mg = (mg.reshape(B, T, 2, 2, MLSTM_HEADS) + ml_gate_bias).transpose(2, 3, 0, 4, 1)
    return dict(
        gq=bhtd(gq) * GLA_DK ** -0.5, gk=bhtd(gk), gv=bhtd(gv.reshape(B, T, GLA_HEADS, GLA_DV)), la=log_a, gg=gg,
        mq=bhtd(mq), mk=bhtd(mk) * MLSTM_DH ** -0.5, mv=bhtd(mv.reshape(B, T, MLSTM_HEADS, MLSTM_DH)),
        li=mg[:, 0], lf=jax.nn.log_sigmoid(mg[:, 1]), mo=mo)


def flip_t(t):
    return jnp.flip(t, axis=2)


def gla_bidir(p, init):
    o_f, s_f = gla_chunked(p['gq'], p['gk'], p['gv'], p['la'][0], init[0])
    o_b, s_b = gla_chunked(flip_t(p['gq']), flip_t(p['gk']), flip_t(p['gv']), flip_t(p['la'][1]), init[1])
    return o_f + flip_t(o_b), (s_f, s_b)


def mlstm_bidir(p, init):
    h_f, s_f = mlstm_chunked(p['mq'], p['mk'], p['mv'], p['li'][0], p['lf'][0], init[0])
    h_b, s_b = mlstm_chunked(flip_t(p['mq']), flip_t(p['mk']), flip_t(p['mv']),
                             flip_t(p['li'][1]), flip_t(p['lf'][1]), init[1])
    return h_f + flip_t(h_b), (s_f, s_b)


def even_output(gla_o, ml_h, p, gla_norm_w, ml_norm_w, w_out, dtype):
    B, _, T, _ = gla_o.shape
    go = head_rms_norm(gla_o.transpose(0, 2, 1, 3), gla_norm_w) * jax.nn.silu(p['gg']).reshape(B, T, GLA_HEADS, GLA_DV)
    mo = head_rms_norm(ml_h.transpose(0, 2, 1, 3), ml_norm_w) * jax.nn.sigmoid(p['mo']).reshape(B, T, MLSTM_HEADS, MLSTM_DH)
    y = jnp.concatenate([go.reshape(B, T, -1), mo.reshape(B, T, -1)], axis=-1).astype(dtype)
    return y @ w_out


def even_mixer(u_lat, u_ctx, w_in, w_out, a_up, a_bias, gla_norm_w, ml_gate_bias, ml_norm_w, rope_gla, rope_ml, need_ctx):
    B = u_lat.shape[0]
    p_ctx = even_project(u_ctx, w_in, a_up, a_bias, ml_gate_bias, None, None)
    p_lat = even_project(u_lat, w_in, a_up, a_bias, ml_gate_bias, rope_gla, rope_ml)
    gla0 = jnp.zeros((B, GLA_HEADS, GLA_DK, GLA_DV), jnp.float32)
    ml0 = (jnp.zeros((B, MLSTM_HEADS, MLSTM_DH, MLSTM_DH), jnp.float32),
           jnp.zeros((B, MLSTM_HEADS, MLSTM_DH), jnp.float32),
           jnp.zeros((B, MLSTM_HEADS), jnp.float32))
    gla_c, gla_states = gla_bidir(p_ctx, (gla0, gla0))
    ml_c, ml_states = mlstm_bidir(p_ctx, (ml0, ml0))
    gla_l, _ = gla_bidir(p_lat, gla_states)
    ml_l, _ = mlstm_bidir(p_lat, ml_states)
    y_lat = even_output(gla_l, ml_l, p_lat, gla_norm_w, ml_norm_w, w_out, u_lat.dtype)
    y_ctx = even_output(gla_c, ml_c, p_ctx, gla_norm_w, ml_norm_w, w_out, u_ctx.dtype) if need_ctx else None
    return y_lat, y_ctx


def neighbourhood_tables(rows):
    kh = min(NA_WIN_H, rows)
    kw = NA_WIN_W
    r = np.arange(rows)
    col = np.arange(GRID_W)
    key_r = np.clip(r - kh // 2, 0, rows - kh)[:, None] + np.arange(kh)[None, :]
    key_c = np.clip(col - kw // 2, 0, GRID_W - kw)[:, None] + np.arange(kw)[None, :]
    kr = key_r[:, None, :, None]
    kc = key_c[None, :, None, :]
    idx = kr * GRID_W + kc
    off = ((kr - r[:, None, None, None] + NA_WIN_H - 1) * (2 * NA_WIN_W - 1)
           + (kc - col[None, :, None, None] + NA_WIN_W - 1))
    n_keys = kh * kw
    return (jnp.asarray(idx.reshape(rows, GRID_W, n_keys), jnp.int32),
            jnp.asarray(off.reshape(rows, GRID_W, n_keys), jnp.int32))


def na_mixer(u_lat, u_ctx, w_qkv, w_out, rpb, need_ctx):
    B, T, D = u_lat.shape
    n_ctx = u_ctx.shape[1]
    rows = T // GRID_W
    scale = NA_DH ** -0.5
    heads = lambda z, n, parts: z.reshape(B, n, parts, NA_HEADS, NA_DH).transpose(2, 0, 3, 1, 4)
    q, k, v = heads(u_lat @ w_qkv, T, 3)
    if need_ctx:
        qc, kc, vc = heads(u_ctx @ w_qkv, n_ctx, 3)
    else:
        kc, vc = heads(u_ctx @ w_qkv[:, D:], n_ctx, 2)
    nbr_idx, bias_idx = neighbourhood_tables(rows)
    n_keys = nbr_idx.shape[-1]
    bias = rpb.reshape(NA_HEADS, -1)[:, bias_idx].astype(jnp.float32).transpose(1, 0, 2, 3)
    q_rows = (q * scale).reshape(B, NA_HEADS, rows, GRID_W, NA_DH).transpose(2, 0, 1, 3, 4)

    def row_block(args):
        q_r, idx_r, bias_r = args
        k_n = k[:, :, idx_r]
        v_n = v[:, :, idx_r]
        s = jnp.concatenate([jnp.einsum('bhqd,bhqkd->bhqk', q_r, k_n).astype(jnp.float32) + bias_r,
                             jnp.einsum('bhqd,bhcd->bhqc', q_r, kc).astype(jnp.float32)], axis=-1)
        p = jax.nn.softmax(s, axis=-1).astype(v.dtype)
        return (jnp.einsum('bhqk,bhqkd->bhqd', p[..., :n_keys], v_n)
                + jnp.einsum('bhqc,bhcd->bhqd', p[..., n_keys:], vc))

    o = lax.map(row_block, (q_rows, nbr_idx, bias))
    y_lat = o.transpose(1, 0, 3, 2, 4).reshape(B, T, D) @ w_out
    y_ctx = None
    if need_ctx:
        s_c = jnp.einsum('bhqd,bhkd->bhqk', qc * scale, kc).astype(jnp.float32)
        oc = jnp.einsum('bhqk,bhkd->bhqd', jax.nn.softmax(s_c, axis=-1).astype(vc.dtype), vc)
        y_ctx = oc.transpose(0, 2, 1, 3).reshape(B, n_ctx, D) @ w_out
    return y_lat, y_ctx


def conv_ffn(u, w_up, conv_w, conv_b, w_down):
    a, g = jnp.split(u @ w_up, 2, axis=-1)
    pad = FFN_CONV // 2
    a = lax.conv_general_dilated(a, conv_w[:, None, :].astype(a.dtype), (1,), [(pad, pad)],
                                 dimension_numbers=('NWC', 'WIO', 'NWC'), feature_group_count=D_FF) + conv_b
    return (jax.nn.gelu(a) * g) @ w_down


def setup_inputs(seed: int = 0) -> dict:
    key = jax.random.key(seed)
    ks = jax.random.split(key, 24)
    n_even = (DEPTH + 1) // 2
    n_odd = DEPTH // 2
    D = D_MODEL
    nrm = lambda k, shape, s: jax.random.normal(k, shape, jnp.float32) * s
    in_gate_b = nrm(ks[12], (n_even, 2, 1, MLSTM_HEADS), 0.1)
    fg_gate_b = jnp.linspace(3.0, 6.0, MLSTM_HEADS, dtype=jnp.float32)[None, None, None, :] + nrm(ks[13], (n_even, 2, 1, MLSTM_HEADS), 0.1)
    return {
        'x': nrm(ks[0], (BATCH, SEQ, D), 1.0),
        'c': nrm(ks[1], (BATCH, D), 1.0),
        'ctx': nrm(ks[2], (BATCH, CTX_LEN, D), 1.0),
        'c_ctx': nrm(ks[3], (D,), 1.0),
        'mod_w': nrm(ks[4], (DEPTH, D, 6 * D), 0.5 * D ** -0.5),
        'mod_b': nrm(ks[5], (DEPTH, 6 * D), 0.02),
        'norm_mix_w': 1.0 + nrm(ks[6], (DEPTH, D), 0.02),
        'norm_ffn_w': 1.0 + nrm(ks[7], (DEPTH, D), 0.02),
        'ev_w_in': nrm(ks[8], (n_even, D, EVEN_IN), D ** -0.5),
        'ev_w_out': nrm(ks[9], (n_even, MIX_WIDTH, D), MIX_WIDTH ** -0.5),
        'gla_a_up': nrm(ks[10], (n_even, 2, GLA_RANK, GLA_HEADS * GLA_DK), GLA_RANK ** -0.5),
        'gla_a_bias': nrm(ks[11], (n_even, 2, GLA_HEADS * GLA_DK), 0.1),
        'gla_norm_w': 1.0 + nrm(ks[14], (n_even, GLA_DV), 0.02),
        'ml_gate_bias': jnp.concatenate([in_gate_b, fg_gate_b], axis=2),
        'ml_norm_w': 1.0 + nrm(ks[15], (n_even, MLSTM_DH), 0.02),
        'na_w_qkv': nrm(ks[16], (n_odd, D, 3 * D), D ** -0.5),
        'na_w_out': nrm(ks[17], (n_odd, D, D), D ** -0.5),
        'na_rpb': nrm(ks[18], (n_odd, NA_HEADS, 2 * NA_WIN_H - 1, 2 * NA_WIN_W - 1), 0.05),
        'ffn_w_up': nrm(ks[19], (DEPTH, D, 2 * D_FF), D ** -0.5),
        'ffn_conv_w': nrm(ks[20], (DEPTH, FFN_CONV, D_FF), FFN_CONV ** -0.5),
        'ffn_conv_b': nrm(ks[21], (DEPTH, D_FF), 0.02),
        'ffn_w_down': nrm(ks[22], (DEPTH, D_FF, D), D_FF ** -0.5),
        'final_norm_w': 1.0 + nrm(ks[23], (D,), 0.02),
    }


def reference(x, c, ctx, c_ctx, mod_w, mod_b, norm_mix_w, norm_ffn_w, ev_w_in, ev_w_out, gla_a_up, gla_a_bias,
              gla_norm_w, ml_gate_bias, ml_norm_w, na_w_qkv, na_w_out, na_rpb, ffn_w_up, ffn_conv_w, ffn_conv_b,
              ffn_w_down, final_norm_w):
    T = x.shape[1]
    rope_gla = axial_rope(T, GLA_DK)
    rope_ml = axial_rope(T, MLSTM_DH)
    h, hc = x, ctx
    for layer in range(DEPTH):
        need_ctx = layer < DEPTH - 1
        j = layer // 2
        m_lat = (jax.nn.silu(c) @ mod_w[layer] + mod_b[layer])[:, None, :]
        m_ctx = (jax.nn.silu(c_ctx) @ mod_w[layer] + mod_b[layer])[None, None, :]
        sh1, sc1, g1, sh2, sc2, g2 = jnp.split(m_lat, 6, axis=-1)
        csh1, csc1, cg1, csh2, csc2, cg2 = jnp.split(m_ctx, 6, axis=-1)
        u = rms_norm(h, norm_mix_w[layer]) * (1.0 + sc1) + sh1
        uc = rms_norm(hc, norm_mix_w[layer]) * (1.0 + csc1) + csh1
        if layer % 2 == 0:
            y, yc = even_mixer(u, uc, ev_w_in[j], ev_w_out[j], gla_a_up[j], gla_a_bias[j], gla_norm_w[j],
                               ml_gate_bias[j], ml_norm_w[j], rope_gla, rope_ml, need_ctx)
        else:
            y, yc = na_mixer(u, uc, na_w_qkv[j], na_w_out[j], na_rpb[j], need_ctx)
        h = h + g1 * y.astype(h.dtype)
        u = rms_norm(h, norm_ffn_w[layer]) * (1.0 + sc2) + sh2
        h = h + g2 * conv_ffn(u, ffn_w_up[layer], ffn_conv_w[layer], ffn_conv_b[layer], ffn_w_down[layer])
        if need_ctx:
            hc = hc + cg1 * yc.astype(hc.dtype)
            uc = rms_norm(hc, norm_ffn_w[layer]) * (1.0 + csc2) + csh2
            hc = hc + cg2 * conv_ffn(uc, ffn_w_up[layer], ffn_conv_w[layer], ffn_conv_b[layer], ffn_w_down[layer])
    return rms_norm(h, final_norm_w)
```

```python
import functools

import numpy as np
import jax
import jax.numpy as jnp
from jax import lax
from jax.experimental import pallas as pl
from jax.experimental.pallas import tpu as pltpu

F32 = jnp.float32
BF16 = jnp.bfloat16

D = 1024
GRID_W = 64
EPS = 1e-6
ROPE_BASE = 10000.0
CHUNK = 64
GLA_H, GLA_DK, GLA_DV, GLA_RANK, GLA_TAU = 4, 64, 128, 16, 16.0
ML_H, ML_DH = 4, 128
NA_H, NA_DH, NA_WIN_H, NA_WIN_W = 16, 64, 8, 16
D_FF = 2816
DEPTH = 4
N_MOD_ROWS = 16

TM = 512
SB = 256
HALO = 16
FC = 256
EVEN_N = 3712
SMALL_COL = 3584
VMEM_LIMIT = 58 * 1024 * 1024


def _dot(a, b):
    return jnp.dot(a, b, preferred_element_type=F32)


def _dot_nt(a, b):
    return lax.dot_general(a, b, (((1,), (1,)), ((), ())), preferred_element_type=F32)


def _dot_tn(a, b):
    return lax.dot_general(a, b, (((0,), (0,)), ((), ())), preferred_element_type=F32)


def _split_bf16(x):
    hi = x.astype(BF16)
    lo = (x - hi.astype(F32)).astype(BF16)
    return hi, lo


def _log_sigmoid(x):
    return jnp.minimum(x, 0.0) - jnp.log(1.0 + jnp.exp(-jnp.abs(x)))


def _sigmoid(x):
    return 1.0 / (1.0 + jnp.exp(-x))


def _params(n_axes=1):
    return pltpu.CompilerParams(dimension_semantics=("arbitrary",) * n_axes, vmem_limit_bytes=VMEM_LIMIT)


def _resident(shape):
    nd = len(shape)
    return pl.BlockSpec(shape, lambda *_: (0,) * nd, pipeline_mode=pl.Buffered(1))


def _mod_kernel(cv_ref, w_ref, b_ref, o_ref):
    a = cv_ref[...]
    a = a * _sigmoid(a)
    w = w_ref[0]
    a_hi, a_lo = _split_bf16(a)
    w_hi, w_lo = _split_bf16(w)
    o_ref[0] = _dot(a_hi, w_hi) + _dot(a_hi, w_lo) + _dot(a_lo, w_hi) + b_ref[0]


def _modulation(cvec, mod_w, mod_b):
    tn = 1536
    return pl.pallas_call(
        _mod_kernel,
        grid=(DEPTH, 6 * D // tn),
        in_specs=[
            pl.BlockSpec((N_MOD_ROWS, D), lambda l, j: (0, 0)),
            pl.BlockSpec((1, D, tn), lambda l, j: (l, 0, j)),
            pl.BlockSpec((1, 1, tn), lambda l, j: (l, 0, j)),
        ],
        out_specs=pl.BlockSpec((1, N_MOD_ROWS, tn), lambda l, j: (l, 0, j)),
        out_shape=jax.ShapeDtypeStruct((DEPTH, N_MOD_ROWS, 6 * D), F32),
        compiler_params=_params(2),
        name="adaln_table",
    )(cvec, mod_w, mod_b.reshape(DEPTH, 1, 6 * D))


def _norm_mod(x, mult, shift):
    ms = jnp.mean(x * x, axis=-1, keepdims=True)
    return x * lax.rsqrt(ms + EPS) * mult + shift


def _mod_row_map(n_lat_tiles, tiles_per_seq, ctx_row):
    def index_map(i):
        return (jnp.where(i < n_lat_tiles, i // tiles_per_seq, ctx_row), 0, 0)
    return index_map


def _rope_gla(x, c, s):
    lane = lax.broadcasted_iota(jnp.int32, x.shape, 1)
    swapped = jnp.where((lane % GLA_DK) < GLA_DK // 2, pltpu.roll(x, 128 - GLA_DK // 2, 1),
                        pltpu.roll(x, GLA_DK // 2, 1))
    return x * c + swapped * s


def _rope_ml(x, c, s):
    return x * c + pltpu.roll(x, ML_DH // 2, 1) * s


def _even_proj_kernel(h_ref, mult_ref, shift_ref, w_ref, cg_ref, sg_ref, cm_ref, sm_ref, z_ref):
    u = _norm_mod(h_ref[...], mult_ref[0], shift_ref[0]).astype(BF16)
    q_scale = {0: GLA_DK ** -0.5, 1: 1.0, 6: 1.0, 7: 1.0, 8: ML_DH ** -0.5, 9: ML_DH ** -0.5}
    for j in range(SMALL_COL // 256):
        z = _dot(u, w_ref[:, j * 256:(j + 1) * 256])
        if j in q_scale:
            rope = _rope_gla if j < 2 else _rope_ml
            c_ref, s_ref = (cg_ref, sg_ref) if j < 2 else (cm_ref, sm_ref)
            for half in range(2):
                zz = rope(z[:, half * 128:(half + 1) * 128], c_ref[...], s_ref[...]) * q_scale[j]
                z_ref[:, j * 256 + half * 128:j * 256 + (half + 1) * 128] = zz
        else:
            z_ref[:, j * 256:(j + 1) * 256] = z
    z_ref[:, SMALL_COL:EVEN_N] = _dot(u, w_ref[:, SMALL_COL:EVEN_N])


def _even_proj(H, mult, shift, w, ropes, n_lat_tiles, tiles_per_seq, ctx_row):
    R = H.shape[0]
    rowmap = _mod_row_map(n_lat_tiles, tiles_per_seq, ctx_row)
    ropemap = lambda i: (jnp.where(i < n_lat_tiles, i % tiles_per_seq, tiles_per_seq), 0)
    return pl.pallas_call(
        _even_proj_kernel,
        grid=(R // TM,),
        in_specs=[
            pl.BlockSpec((TM, D), lambda i: (i, 0)),
            pl.BlockSpec((1, 1, D), rowmap),
            pl.BlockSpec((1, 1, D), rowmap),
            _resident((D, EVEN_N)),
        ] + [pl.BlockSpec((TM, 128), ropemap)] * 4,
        out_specs=pl.BlockSpec((TM, EVEN_N), lambda i: (i, 0)),
        out_shape=jax.ShapeDtypeStruct((R, EVEN_N), F32),
        compiler_params=_params(),
        name="even_in_proj",
    )(H, mult, shift, w, *ropes)


def _scan_chunk(z, r0, refs, aup_ref, abias_ref, mgb_ref, out_ref, s_ref, c_ref, n_ref, m_ref):
    gq_ref, gk_ref, gv_ref, sm_ref, mq_ref, mk_ref, mv_ref = refs
    rows = pl.ds(r0, CHUNK)
    ti = lax.broadcasted_iota(jnp.int32, (CHUNK, CHUNK), 0)
    si = lax.broadcasted_iota(jnp.int32, (CHUNK, CHUNK), 1)
    keep = (si <= ti) if z == 0 else (si >= ti)
    tri = keep.astype(BF16)
    tri_t = ((ti <= si) if z == 0 else (ti >= si)).astype(BF16)
    edge = CHUNK - 1 if z == 0 else 0

    small = sm_ref[rows, :]
    a_pre = _dot(small.astype(BF16), aup_ref[z].astype(BF16)) + abias_ref[z]
    la = _log_sigmoid(a_pre) * (1.0 / GLA_TAU)
    la_hi, la_lo = _split_bf16(la)
    b = _dot(tri, la_hi) + _dot(tri, la_lo)
    g = b[edge:edge + 1, :]
    q = gq_ref[rows, :]
    k = gk_ref[rows, :]
    q_dec = (q * jnp.exp(b)).astype(BF16)
    k_dec = (k * jnp.exp(-b)).astype(BF16)
    k_end = (k * jnp.exp(g - b)).astype(BF16)
    decay = jnp.exp(g)
    lane = lax.broadcasted_iota(jnp.int32, (CHUNK, GLA_H * GLA_DK), 1)
    for h in range(GLA_H):
        in_head = (lane // GLA_DK) == h
        qm = jnp.where(in_head, q_dec, jnp.zeros_like(q_dec))
        km = jnp.where(in_head, k_end, jnp.zeros_like(k_end))
        v = gv_ref[rows, h * GLA_DV:(h + 1) * GLA_DV].astype(BF16)
        scores = jnp.where(keep, _dot_nt(qm, k_dec), 0.0).astype(BF16)
        state = s_ref[z, h]
        o = _dot(scores, v) + _dot_nt(qm, state.astype(BF16))
        out_ref[rows, h * GLA_DV:(h + 1) * GLA_DV] = o
        s_ref[z, h] = decay * state + _dot_tn(v, km)

    gates = small + mgb_ref[...]
    lf = _log_sigmoid(gates)
    lf_hi, lf_lo = _split_bf16(lf)
    b_col = _dot(tri, lf_hi) + _dot(tri, lf_lo)
    gates_t = gates.T
    lf_t = lf.T
    lft_hi, lft_lo = _split_bf16(lf_t)
    b_row = _dot(lft_hi, tri_t) + _dot(lft_lo, tri_t)
    for h in range(ML_H):
        fi = 2 * GLA_RANK + z * 2 * ML_H + h
        ff = fi + ML_H
        li_r = gates_t[fi:fi + 1, :]
        li_c = gates[:, fi:fi + 1]
        b_r = b_row[ff:ff + 1, :]
        b_c = b_col[:, ff:ff + 1]
        g1 = b_c[edge:edge + 1, :]
        log_d = jnp.where(keep, b_c - b_r + li_r, -jnp.inf)
        m_loc = jnp.max(g1 - b_r + li_r, axis=1, keepdims=True)
        w_c = jnp.exp(g1 - b_c + li_c - m_loc)
        mrow = z * ML_H + h
        m_old = m_ref[mrow:mrow + 1, 0:1]
        c_old = c_ref[z, h]
        n_old = n_ref[z, h]
        qh = mq_ref[rows, h * ML_DH:(h + 1) * ML_DH]
        kh = mk_ref[rows, h * ML_DH:(h + 1) * ML_DH]
        vh = mv_ref[rows, h * ML_DH:(h + 1) * ML_DH].astype(BF16)
        qb = qh.astype(BF16)
        log_inter = b_c + m_old
        m_t = jnp.maximum(log_inter, jnp.max(log_d, axis=1, keepdims=True))
        d_w = jnp.exp(log_d - m_t)
        inter_w = jnp.exp(log_inter - m_t)
        s = _dot_nt(qb, kh.astype(BF16)) * d_w
        numer = _dot(s.astype(BF16), vh) + inter_w * _dot(qb, c_old.astype(BF16))
        denom = jnp.sum(s, axis=1, keepdims=True) + inter_w * jnp.sum(qh * n_old, axis=1, keepdims=True)
        hh = numer / jnp.maximum(jnp.abs(denom), jnp.exp(-m_t))
        col = GLA_H * GLA_DV + h * ML_DH
        out_ref[rows, col:col + ML_DH] = hh
        kw = kh * w_c
        m_new = jnp.maximum(g1 + m_old, m_loc)
        a_sc = jnp.exp(g1 + m_old - m_new)
        w_sc = jnp.exp(m_loc - m_new)
        c_ref[z, h] = a_sc * c_old + w_sc * _dot_tn(kw.astype(BF16), vh)
        n_ref[z, h] = a_sc * n_old + w_sc * jnp.sum(kw, axis=0, keepdims=True)
        m_ref[mrow:mrow + 1, :] = jnp.broadcast_to(m_new, (1, 128))


def _scan_kernel(*args):
    fwd, bwd = args[0:7], args[7:14]
    aup_ref, abias_ref, mgb_ref, of_ref, ob_ref, s_ref, c_ref, n_ref, m_ref = args[14:]

    @pl.when(pl.program_id(1) == 0)
    def _():
        s_ref[...] = jnp.zeros_like(s_ref)
        c_ref[...] = jnp.zeros_like(c_ref)
        n_ref[...] = jnp.zeros_like(n_ref)
        m_ref[...] = jnp.zeros_like(m_ref)

    n_chunks = SB // CHUNK

    def body(ci, carry):
        _scan_chunk(0, pl.multiple_of(ci * CHUNK, CHUNK), fwd, aup_ref, abias_ref, mgb_ref, of_ref,
                    s_ref, c_ref, n_ref, m_ref)
        _scan_chunk(1, pl.multiple_of((n_chunks - 1 - ci) * CHUNK, CHUNK), bwd, aup_ref, abias_ref, mgb_ref,
                    ob_ref, s_ref, c_ref, n_ref, m_ref)
        return carry

    lax.fori_loop(0, n_chunks, body, 0)


def _scan(Z, aup, abias, mgb, B, T, LC):
    R = Z.shape[0]
    assert LC == SB and T % SB == 0
    nb = T // SB
    ctx0 = B * nb
    fmap = lambda b, s: jnp.where(s == 0, ctx0 + b, b * nb + s - 1)
    bmap = lambda b, s: jnp.where(s == 0, ctx0 + b, b * nb + nb - s)

    def specs(rowmap):
        col = lambda width, blk: pl.BlockSpec((SB, width), lambda b, s: (rowmap(b, s), blk))
        return [col(256, 0), col(256, 1), col(512, 1), col(128, SMALL_COL // 128),
                col(512, 3), col(512, 4), col(512, 5)]

    out_spec = lambda rowmap: pl.BlockSpec((SB, D), lambda b, s: (rowmap(b, s), 0))
    return pl.pallas_call(
        _scan_kernel,
        grid=(B, nb + 1),
        in_specs=specs(fmap) + specs(bmap) + [
            pl.BlockSpec((2, 128, 256), lambda b, s: (0, 0, 0)),
            pl.BlockSpec((2, 1, 256), lambda b, s: (0, 0, 0)),
            pl.BlockSpec((1, 128), lambda b, s: (0, 0)),
        ],
        out_specs=[out_spec(fmap), out_spec(bmap)],
        out_shape=[jax.ShapeDtypeStruct((R, D), F32)] * 2,
        scratch_shapes=[
            pltpu.VMEM((2, GLA_H, GLA_DV, GLA_H * GLA_DK), F32),
            pltpu.VMEM((2, ML_H, ML_DH, ML_DH), F32),
            pltpu.VMEM((2, ML_H, 1, ML_DH), F32),
            pltpu.VMEM((2 * ML_H, 128), F32),
        ],
        compiler_params=_params(2),
        name="gla_mlstm_scan",
    )(*([Z] * 14), aup, abias, mgb)


def _even_out_kernel(of_ref, ob_ref, gg_ref, mo_ref, h_ref, gate_ref, gnw_ref, mnw_ref, w_ref, o_ref, y_ref):
    for hh in range(GLA_H + ML_H):
        cols = slice(hh * 128, (hh + 1) * 128)
        x = of_ref[:, cols] + ob_ref[:, cols]
        ms = jnp.mean(x * x, axis=-1, keepdims=True)
        if hh < GLA_H:
            gt = gg_ref[:, cols]
            y = x * lax.rsqrt(ms + EPS) * gnw_ref[...] * (gt * _sigmoid(gt))
        else:
            gt = mo_ref[:, (hh - GLA_H) * 128:(hh - GLA_H + 1) * 128]
            y = x * lax.rsqrt(ms + EPS) * mnw_ref[...] * _sigmoid(gt)
        y_ref[:, cols] = y.astype(BF16)
    o_ref[...] = h_ref[...] + gate_ref[0] * _dot(y_ref[...], w_ref[...])


def _even_out(of, ob, Z, H, gate, gnw, mnw, w, n_lat_tiles, tiles_per_seq, ctx_row):
    R = H.shape[0]
    rowmap = _mod_row_map(n_lat_tiles, tiles_per_seq, ctx_row)
    return pl.pallas_call(
        _even_out_kernel,
        grid=(R // TM,),
        in_specs=[
            pl.BlockSpec((TM, D), lambda i: (i, 0)),
            pl.BlockSpec((TM, D), lambda i: (i, 0)),
            pl.BlockSpec((TM, 512), lambda i: (i, 2)),
            pl.BlockSpec((TM, 512), lambda i: (i, 6)),
            pl.BlockSpec((TM, D), lambda i: (i, 0)),
            pl.BlockSpec((1, 1, D), rowmap),
            pl.BlockSpec((1, 128), lambda i: (0, 0)),
            pl.BlockSpec((1, 128), lambda i: (0, 0)),
            _resident((D, D)),
        ],
        out_specs=pl.BlockSpec((TM, D), lambda i: (i, 0)),
        out_shape=jax.ShapeDtypeStruct((R, D), F32),
        scratch_shapes=[pltpu.VMEM((TM, D), BF16)],
        input_output_aliases={4: 0},
        compiler_params=_params(),
        name="even_out_proj",
    )(of, ob, Z, Z, H, gate, gnw, mnw, w)


def _na_proj_kernel(h_ref, mult_ref, shift_ref, w_ref, z_ref):
    u = _norm_mod(h_ref[...], mult_ref[0], shift_ref[0]).astype(BF16)
    for j in range(3 * D // 256):
        z = _dot(u, w_ref[:, j * 256:(j + 1) * 256])
        if j < D // 256:
            z = z * (NA_DH ** -0.5)
        z_ref[:, j * 256:(j + 1) * 256] = z.astype(BF16)


def _na_proj(H, mult, shift, w, n_lat_tiles, tiles_per_seq, ctx_row):
    R = H.shape[0]
    rowmap = _mod_row_map(n_lat_tiles, tiles_per_seq, ctx_row)
    return pl.pallas_call(
        _na_proj_kernel,
        grid=(R // TM,),
        in_specs=[
            pl.BlockSpec((TM, D), lambda i: (i, 0)),
            pl.BlockSpec((1, 1, D), rowmap),
            pl.BlockSpec((1, 1, D), rowmap),
            _resident((D, 3 * D)),
        ],
        out_specs=pl.BlockSpec((TM, 3 * D), lambda i: (i, 0)),
        out_shape=jax.ShapeDtypeStruct((R, 3 * D), BF16),
        compiler_params=_params(),
        name="na_qkv_proj",
    )(H, mult, shift, w)


def _softmax_pv(s_list, v_list):
    m = functools.reduce(jnp.maximum, [jnp.max(s, axis=1, keepdims=True) for s in s_list])
    p_list = [jnp.exp(s - m) for s in s_list]
    l = functools.reduce(jnp.add, [jnp.sum(p, axis=1, keepdims=True) for p in p_list])
    o = functools.reduce(jnp.add, [_dot(p.astype(BF16), v) for p, v in zip(p_list, v_list)])
    return o / l


def _na_kernel(q_ref, k_ref, v_ref, kc_ref, vc_ref, qc_ref, tb_ref, o_ref, oc_ref, *, rows):
    kc = kc_ref[...]
    vc = vc_ref[...]
    win = NA_WIN_H * GRID_W
    lane = lax.broadcasted_iota(jnp.int32, (GRID_W, 128), 1)

    def row_body(r, carry):
        kr0 = jnp.clip(r - NA_WIN_H // 2, 0, rows - NA_WIN_H)
        shift = r - kr0
        q = q_ref[pl.ds(pl.multiple_of(r * GRID_W, GRID_W), GRID_W), :]
        krows = pl.ds(pl.multiple_of(kr0 * GRID_W, GRID_W), win)
        kw = k_ref[krows, :]
        vw = v_ref[krows, :]
        outs = []
        for hh in range(2):
            qm = jnp.where((lane // NA_DH) == hh, q, jnp.zeros_like(q))
            s_w = _dot_nt(qm, kw) + tb_ref[hh, shift]
            s_c = _dot_nt(qm, kc)
            outs.append(_softmax_pv([s_w, s_c], [vw, vc]))
        o = jnp.where(lane < NA_DH, outs[0], outs[1])
        o_ref[pl.ds(pl.multiple_of(r * GRID_W, GRID_W), GRID_W), :] = o.astype(o_ref.dtype)
        return carry

    lax.fori_loop(0, rows, row_body, 0)

    qc = qc_ref[...]
    lane_c = lax.broadcasted_iota(jnp.int32, qc.shape, 1)
    outs = []
    for hh in range(2):
        qm = jnp.where((lane_c // NA_DH) == hh, qc, jnp.zeros_like(qc))
        outs.append(_softmax_pv([_dot_nt(qm, kc)], [vc]))
    oc_ref[...] = jnp.where(lane_c < NA_DH, outs[0], outs[1]).astype(oc_ref.dtype)


def _na_attention(Zq, tb, B, T, LC):
    rows = T // GRID_W
    n_pairs = NA_H // 2
    ctx_blk0 = B * T // LC
    lat = lambda part: pl.BlockSpec((T, 128), lambda b, p: (b, part * n_pairs + p))
    ctx = lambda part: pl.BlockSpec((LC, 128), lambda b, p: (ctx_blk0 + b, part * n_pairs + p))
    return pl.pallas_call(
        functools.partial(_na_kernel, rows=rows),
        grid=(B, n_pairs),
        in_specs=[lat(0), lat(1), lat(2), ctx(1), ctx(2), ctx(0),
                  pl.BlockSpec((2, NA_WIN_H, GRID_W, NA_WIN_H * GRID_W), lambda b, p: (p, 0, 0, 0))],
        out_specs=[pl.BlockSpec((T, 128), lambda b, p: (b, p)),
                   pl.BlockSpec((LC, 128), lambda b, p: (b, p))],
        out_shape=[jax.ShapeDtypeStruct((B * T, D), BF16), jax.ShapeDtypeStruct((B * LC, D), BF16)],
        compiler_params=_params(2),
        name="na_attention",
    )(Zq, Zq, Zq, Zq, Zq, Zq, tb)


def _na_bias_tables(rpb):
    c = np.arange(GRID_W)[:, None]
    kc = np.arange(GRID_W)[None, :]
    start = np.clip(c - NA_WIN_W // 2, 0, GRID_W - NA_WIN_W)
    inside = (kc >= start) & (kc < start + NA_WIN_W)
    dc = np.clip(kc - c + NA_WIN_W - 1, 0, 2 * NA_WIN_W - 2)
    shift = np.arange(NA_WIN_H)[:, None]
    j = np.arange(NA_WIN_H)[None, :]
    dr = j - shift + NA_WIN_H - 1
    vals = rpb.astype(F32)[:, dr[:, :, None, None], dc[None, None, :, :]]
    vals = jnp.where(inside[None, None, None], vals, -1e30)
    return vals.transpose(0, 1, 3, 2, 4).reshape(NA_H, NA_WIN_H, GRID_W, NA_WIN_H * GRID_W)


def _na_out_kernel(y_ref, h_ref, gate_ref, w_ref, o_ref):
    o_ref[...] = h_ref[...] + gate_ref[0] * _dot(y_ref[...], w_ref[...])


def _na_out(Y, H, gate, w, n_lat_tiles, tiles_per_seq, ctx_row):
    R = H.shape[0]
    rowmap = _mod_row_map(n_lat_tiles, tiles_per_seq, ctx_row)
    return pl.pallas_call(
        _na_out_kernel,
        grid=(R // TM,),
        in_specs=[
            pl.BlockSpec((TM, D), lambda i: (i, 0)),
            pl.BlockSpec((TM, D), lambda i: (i, 0)),
            pl.BlockSpec((1, 1, D), rowmap),
            _resident((D, D)),
        ],
        out_specs=pl.BlockSpec((TM, D), lambda i: (i, 0)),
        out_shape=jax.ShapeDtypeStruct((R, D), F32),
        input_output_aliases={1: 0},
        compiler_params=_params(),
        name="na_out_proj",
    )(Y, H, gate, w)


def _ffn_kernel(hp_ref, h_ref, hn_ref, mult_ref, shift_ref, gate_ref, wup_ref, cw_ref, cb_ref, wd_ref, fnw_ref,
                o_ref, u_ref, act_ref, *, final, n_lat_rows, T, LC):
    x = h_ref[...]
    mult = mult_ref[0]
    shift = shift_ref[0]
    u_ref[0:HALO, :] = _norm_mod(hp_ref[...], mult, shift).astype(BF16)
    u_ref[HALO:HALO + TM, :] = _norm_mod(x, mult, shift).astype(BF16)
    u_ref[HALO + TM:, :] = _norm_mod(hn_ref[...], mult, shift).astype(BF16)

    r = pl.program_id(0) * TM + lax.broadcasted_iota(jnp.int32, (TM, 1), 0)
    is_lat = r < n_lat_rows
    pos = jnp.where(is_lat, r & (T - 1), r & (LC - 1))
    first = pos == 0
    last = pos == jnp.where(is_lat, T - 1, LC - 1)

    n_ext = TM + 2 * HALO
    for c in range(D_FF // FC):
        cols = slice(c * FC, (c + 1) * FC)
        a = _dot(u_ref[...], wup_ref[:, cols])
        a_prev = pltpu.roll(a, 1, 0)[HALO:HALO + TM]
        a_next = pltpu.roll(a, n_ext - 1, 0)[HALO:HALO + TM]
        a_mid = a[HALO:HALO + TM]
        conv = (cw_ref[0:1, cols] * jnp.where(first, 0.0, a_prev) + cw_ref[1:2, cols] * a_mid
                + cw_ref[2:3, cols] * jnp.where(last, 0.0, a_next) + cb_ref[:, cols])
        gt = _dot(u_ref[HALO:HALO + TM, :], wup_ref[:, D_FF + c * FC:D_FF + (c + 1) * FC])
        act_ref[:, cols] = (jax.nn.gelu(conv, approximate=True) * gt).astype(BF16)

    y = x + gate_ref[0] * _dot(act_ref[...], wd_ref[...])
    if final:
        y = _norm_mod(y, fnw_ref[...], 0.0)
    o_ref[...] = y


def _ffn(H, mult, shift, gate, wup, cw, cb, wd, fnw, *, final, n_rows_out, n_lat_tiles, tiles_per_seq,
         ctx_row, n_lat_rows, T, LC):
    R = H.shape[0]
    rowmap = _mod_row_map(n_lat_tiles, tiles_per_seq, ctx_row)
    per = TM // HALO
    last_halo_blk = R // HALO - 1
    kern = functools.partial(_ffn_kernel, final=final, n_lat_rows=n_lat_rows, T=T, LC=LC)
    return pl.pallas_call(
        kern,
        grid=(n_rows_out // TM,),
        in_specs=[
            pl.BlockSpec((HALO, D), lambda i: (jnp.maximum(i * per - 1, 0), 0)),
            pl.BlockSpec((TM, D), lambda i: (i, 0)),
            pl.BlockSpec((HALO, D), lambda i: (jnp.minimum((i + 1) * per, last_halo_blk), 0)),
            pl.BlockSpec((1, 1, D), rowmap),
            pl.BlockSpec((1, 1, D), rowmap),
            pl.BlockSpec((1, 1, D), rowmap),
            _resident((D, 2 * D_FF)),
            pl.BlockSpec((3, D_FF), lambda i: (0, 0)),
            pl.BlockSpec((1, D_FF), lambda i: (0, 0)),
            _resident((D_FF, D)),
            pl.BlockSpec((1, D), lambda i: (0, 0)),
        ],
        out_specs=pl.BlockSpec((TM, D), lambda i: (i, 0)),
        out_shape=jax.ShapeDtypeStruct((n_rows_out, D), F32),
        scratch_shapes=[pltpu.VMEM((TM + 2 * HALO, D), BF16), pltpu.VMEM((TM, D_FF), BF16)],
        compiler_params=_params(),
        name="conv_ffn_final" if final else "conv_ffn",
    )(H, H, H, mult, shift, gate, wup, cw, cb, wd, fnw)


def _rope_tables(T, head_dim):
    pos = jnp.arange(T, dtype=jnp.int32)
    row = (pos // GRID_W).astype(F32)
    col = (pos % GRID_W).astype(F32)
    n_freq = head_dim // 4
    inv_freq = ROPE_BASE ** (-jnp.arange(n_freq, dtype=F32) / n_freq)
    ang = jnp.concatenate([row[:, None] * inv_freq, col[:, None] * inv_freq], axis=-1)
    cos, sin = jnp.cos(ang), jnp.sin(ang)
    reps = 128 // head_dim
    c = jnp.tile(jnp.concatenate([cos, cos], axis=-1), (1, reps))
    s = jnp.tile(jnp.concatenate([-sin, sin], axis=-1), (1, reps))
    c = jnp.concatenate([c, jnp.ones((TM, 128), F32)], axis=0)
    s = jnp.concatenate([s, jnp.zeros((TM, 128), F32)], axis=0)
    return c, s


def kernel(x, c, ctx, c_ctx, mod_w, mod_b, norm_mix_w, norm_ffn_w, ev_w_in, ev_w_out, gla_a_up, gla_a_bias,
           gla_norm_w, ml_gate_bias, ml_norm_w, na_w_qkv, na_w_out, na_rpb, ffn_w_up, ffn_conv_w, ffn_conv_b,
           ffn_w_down, final_norm_w):
    B, T, _ = x.shape
    LC = ctx.shape[1]
    assert x.shape[2] == D and T % TM == 0 and (B * LC) % TM == 0 and B < N_MOD_ROWS
    assert T & (T - 1) == 0 and LC & (LC - 1) == 0 and T % GRID_W == 0
    n_lat_rows = B * T
    n_lat_tiles = n_lat_rows // TM
    tiles_per_seq = T // TM
    tile_kw = dict(n_lat_tiles=n_lat_tiles, tiles_per_seq=tiles_per_seq, ctx_row=B)

    H = jnp.concatenate([x.reshape(B * T, D), ctx.reshape(B * LC, D)], axis=0)
    R = H.shape[0]

    cvec = jnp.zeros((N_MOD_ROWS, D), F32).at[:B].set(c).at[B].set(c_ctx)
    mods = _modulation(cvec, mod_w, mod_b)
    ropes = _rope_tables(T, GLA_DK) + _rope_tables(T, ML_DH)

    out = None
    for layer in range(DEPTH):
        j = layer // 2
        last = layer == DEPTH - 1
        sh1, sc1, g1, sh2, sc2, g2 = [m.reshape(N_MOD_ROWS, 1, D) for m in jnp.split(mods[layer], 6, axis=-1)]
        mult1 = norm_mix_w[layer] * (1.0 + sc1)
        mult2 = norm_ffn_w[layer] * (1.0 + sc2)
        if layer % 2 == 0:
            w = ev_w_in[j]
            w_re = jnp.concatenate(
                [w[:, 0:1536], w[:, 1568:3616], w[:, 1536:1568], w[:, 3616:3632],
                 jnp.zeros((D, EVEN_N - 3632), F32)], axis=1).astype(BF16)
            Z = _even_proj(H, mult1, sh1, w_re, ropes, **tile_kw)
            aup = jnp.zeros((2, 128, GLA_H * GLA_DK), F32)
            aup = aup.at[0, 0:GLA_RANK].set(gla_a_up[j, 0]).at[1, GLA_RANK:2 * GLA_RANK].set(gla_a_up[j, 1])
            mgb = jnp.zeros((1, 128), F32).at[0, 2 * GLA_RANK:2 * GLA_RANK + 4 * ML_H].set(ml_gate_bias[j].reshape(-1))
            of, ob = _scan(Z, aup, gla_a_bias[j].reshape(2, 1, -1), mgb, B, T, LC)
            H = _even_out(of, ob, Z, H, g1, gla_norm_w[j].reshape(1, -1), ml_norm_w[j].reshape(1, -1),
                          ev_w_out[j].astype(BF16), **tile_kw)
        else:
            Zq = _na_proj(H, mult1, sh1, na_w_qkv[j].astype(BF16), **tile_kw)
            o_lat, o_ctx = _na_attention(Zq, _na_bias_tables(na_rpb[j]), B, T, LC)
            H = _na_out(jnp.concatenate([o_lat, o_ctx], axis=0), H, g1, na_w_out[j].astype(BF16), **tile_kw)
        res = _ffn(H, mult2, sh2, g2, ffn_w_up[layer].astype(BF16), ffn_conv_w[layer],
                   ffn_conv_b[layer].reshape(1, -1), ffn_w_down[layer].astype(BF16), final_norm_w.reshape(1, -1),
                   final=last, n_rows_out=n_lat_rows if last else R, n_lat_rows=n_lat_rows, T=T, LC=LC, **tile_kw)
        if last:
            out = res
        else:
            H = res
    return out.reshape(B, T, D)
```

```python
import functools

import numpy as np
import jax
import jax.numpy as jnp
from jax import lax
from jax.experimental import pallas as pl
from jax.experimental.pallas import tpu as pltpu

F32 = jnp.float32
BF16 = jnp.bfloat16

D = 1024
GRID_W = 64
EPS = 1e-6
ROPE_BASE = 10000.0
CHUNK = 64
GLA_H, GLA_DK, GLA_DV, GLA_RANK, GLA_TAU = 4, 64, 128, 16, 16.0
ML_H, ML_DH = 4, 128
NA_H, NA_DH, NA_WIN_H, NA_WIN_W = 16, 64, 8, 16
D_FF = 2816
DEPTH = 4
N_MOD_ROWS = 16

TM = 512
SB = 256
HALO = 16
FC = 256
EVEN_N = 3712
SMALL_COL = 3584
VMEM_LIMIT = 58 * 1024 * 1024


def _dot(a, b):
    return jnp.dot(a, b, preferred_element_type=F32)


def _dot_nt(a, b):
    return lax.dot_general(a, b, (((1,), (1,)), ((), ())), preferred_element_type=F32)


def _dot_tn(a, b):
    return lax.dot_general(a, b, (((0,), (0,)), ((), ())), preferred_element_type=F32)


def _split_bf16(x):
    hi = x.astype(BF16)
    lo = (x - hi.astype(F32)).astype(BF16)
    return hi, lo


def _log_sigmoid(x):
    return jnp.minimum(x, 0.0) - jnp.log(1.0 + jnp.exp(-jnp.abs(x)))


def _sigmoid(x):
    return 1.0 / (1.0 + jnp.exp(-x))


def _params(n_axes=1):
    return pltpu.CompilerParams(dimension_semantics=("arbitrary",) * n_axes, vmem_limit_bytes=VMEM_LIMIT)


def _resident(shape):
    nd = len(shape)
    return pl.BlockSpec(shape, lambda *_: (0,) * nd, pipeline_mode=pl.Buffered(1))


def _mod_kernel(cv_ref, w_ref, b_ref, o_ref):
    a = cv_ref[...]
    a = a * _sigmoid(a)
    w = w_ref[0]
    a_hi, a_lo = _split_bf16(a)
    w_hi, w_lo = _split_bf16(w)
    o_ref[0] = _dot(a_hi, w_hi) + _dot(a_hi, w_lo) + _dot(a_lo, w_hi) + b_ref[0]


def _modulation(cvec, mod_w, mod_b):
    tn = 1536
    return pl.pallas_call(
        _mod_kernel,
        grid=(DEPTH, 6 * D // tn),
        in_specs=[
            pl.BlockSpec((N_MOD_ROWS, D), lambda l, j: (0, 0)),
            pl.BlockSpec((1, D, tn), lambda l, j: (l, 0, j)),
            pl.BlockSpec((1, 1, tn), lambda l, j: (l, 0, j)),
        ],
        out_specs=pl.BlockSpec((1, N_MOD_ROWS, tn), lambda l, j: (l, 0, j)),
        out_shape=jax.ShapeDtypeStruct((DEPTH, N_MOD_ROWS, 6 * D), F32),
        compiler_params=_params(2),
        name="adaln_table",
    )(cvec, mod_w, mod_b.reshape(DEPTH, 1, 6 * D))


def _norm_mod(x, mult, shift):
    ms = jnp.mean(x * x, axis=-1, keepdims=True)
    return x * lax.rsqrt(ms + EPS) * mult + shift


def _mod_row_map(n_lat_tiles, tiles_per_seq, ctx_row):
    def index_map(i):
        return (jnp.where(i < n_lat_tiles, i // tiles_per_seq, ctx_row), 0, 0)
    return index_map


def _rope_gla(x, c, s):
    lane = lax.broadcasted_iota(jnp.int32, x.shape, 1)
    swapped = jnp.where((lane % GLA_DK) < GLA_DK // 2, pltpu.roll(x, 128 - GLA_DK // 2, 1),
                        pltpu.roll(x, GLA_DK // 2, 1))
    return x * c + swapped * s


def _rope_ml(x, c, s):
    return x * c + pltpu.roll(x, ML_DH // 2, 1) * s


def _even_proj_kernel(h_ref, mult_ref, shift_ref, w_ref, cg_ref, sg_ref, cm_ref, sm_ref, z_ref):
    u = _norm_mod(h_ref[...], mult_ref[0], shift_ref[0]).astype(BF16)
    q_scale = {0: GLA_DK ** -0.5, 1: 1.0, 6: 1.0, 7: 1.0, 8: ML_DH ** -0.5, 9: ML_DH ** -0.5}
    for j in range(SMALL_COL // 256):
        z = _dot(u, w_ref[:, j * 256:(j + 1) * 256])
        if j in q_scale:
            rope = _rope_gla if j < 2 else _rope_ml
            c_ref, s_ref = (cg_ref, sg_ref) if j < 2 else (cm_ref, sm_ref)
            for half in range(2):
                zz = rope(z[:, half * 128:(half + 1) * 128], c_ref[...], s_ref[...]) * q_scale[j]
                z_ref[:, j * 256 + half * 128:j * 256 + (half + 1) * 128] = zz
        else:
            z_ref[:, j * 256:(j + 1) * 256] = z
    z_ref[:, SMALL_COL:EVEN_N] = _dot(u, w_ref[:, SMALL_COL:EVEN_N])


def _even_proj(H, mult, shift, w, ropes, n_lat_tiles, tiles_per_seq, ctx_row):
    R = H.shape[0]
    rowmap = _mod_row_map(n_lat_tiles, tiles_per_seq, ctx_row)
    ropemap = lambda i: (jnp.where(i < n_lat_tiles, i % tiles_per_seq, tiles_per_seq), 0)
    return pl.pallas_call(
        _even_proj_kernel,
        grid=(R // TM,),
        in_specs=[
            pl.BlockSpec((TM, D), lambda i: (i, 0)),
            pl.BlockSpec((1, 1, D), rowmap),
            pl.BlockSpec((1, 1, D), rowmap),
            _resident((D, EVEN_N)),
        ] + [pl.BlockSpec((TM, 128), ropemap)] * 4,
        out_specs=pl.BlockSpec((TM, EVEN_N), lambda i: (i, 0)),
        out_shape=jax.ShapeDtypeStruct((R, EVEN_N), F32),
        compiler_params=_params(),
        name="even_in_proj",
    )(H, mult, shift, w, *ropes)


def _scan_pair(rows2, in_refs, out_refs, aup_ref, abias_ref, mgb_ref, s_ref, c_ref, n_ref, m_ref, consts):
    keep, tri, tri_t, in_head = consts
    dirs = (0, 1)
    zh_g = [(z, h) for z in dirs for h in range(GLA_H)]
    zh_m = [(z, h) for z in dirs for h in range(ML_H)]
    edge = (CHUNK - 1, 0)
    gq_ref, gk_ref, gv_ref, sm_ref, mq_ref, mk_ref, mv_ref = zip(*in_refs)

    small = [sm_ref[z][rows2[z], :] for z in dirs]
    a_pre = [_dot(small[z].astype(BF16), aup_ref[z]) + abias_ref[z] for z in dirs]
    gates = [small[z] + mgb_ref[...] for z in dirs]
    lf = [_log_sigmoid(x) for x in gates]
    lf_s = [_split_bf16(x) for x in lf]
    b_col = [_dot(tri[z], lf_s[z][0]) + _dot(tri[z], lf_s[z][1]) for z in dirs]
    gates_t = [x.T for x in gates]
    lft_s = [_split_bf16(x.T) for x in lf]
    b_row = [_dot(lft_s[z][0], tri_t[z]) + _dot(lft_s[z][1], tri_t[z]) for z in dirs]
    la = [_log_sigmoid(x) * (1.0 / GLA_TAU) for x in a_pre]
    la_s = [_split_bf16(x) for x in la]
    b = [_dot(tri[z], la_s[z][0]) + _dot(tri[z], la_s[z][1]) for z in dirs]

    g = [b[z][edge[z]:edge[z] + 1, :] for z in dirs]
    q = [gq_ref[z][rows2[z], :] for z in dirs]
    k = [gk_ref[z][rows2[z], :] for z in dirs]
    q_dec = [(q[z] * jnp.exp(b[z])).astype(BF16) for z in dirs]
    k_dec = [(k[z] * jnp.exp(-b[z])).astype(BF16) for z in dirs]
    k_end = [(k[z] * jnp.exp(g[z] - b[z])).astype(BF16) for z in dirs]
    decay = [jnp.exp(x) for x in g]
    qm = {(z, h): jnp.where(in_head[h], q_dec[z], jnp.zeros_like(q_dec[z])) for z, h in zh_g}
    km = {(z, h): jnp.where(in_head[h], k_end[z], jnp.zeros_like(k_end[z])) for z, h in zh_g}
    gv = {(z, h): gv_ref[z][rows2[z], h * GLA_DV:(h + 1) * GLA_DV].astype(BF16) for z, h in zh_g}
    s_old = {(z, h): s_ref[z, h] for z, h in zh_g}

    ml = {}
    for z, h in zh_m:
        fi = 2 * GLA_RANK + z * 2 * ML_H + h
        ff = fi + ML_H
        li_r = gates_t[z][fi:fi + 1, :]
        li_c = gates[z][:, fi:fi + 1]
        b_r = b_row[z][ff:ff + 1, :]
        b_c = b_col[z][:, ff:ff + 1]
        g1 = b_c[edge[z]:edge[z] + 1, :]
        log_d = jnp.where(keep[z], b_c - b_r + li_r, -jnp.inf)
        m_loc = jnp.max(g1 - b_r + li_r, axis=1, keepdims=True)
        w_c = jnp.exp(g1 - b_c + li_c - m_loc)
        mrow = z * ML_H + h
        m_old = m_ref[mrow:mrow + 1, 0:1]
        log_inter = b_c + m_old
        m_t = jnp.maximum(log_inter, jnp.max(log_d, axis=1, keepdims=True))
        cols = slice(h * ML_DH, (h + 1) * ML_DH)
        qh = mq_ref[z][rows2[z], cols]
        kh = mk_ref[z][rows2[z], cols]
        ml[z, h] = dict(
            g1=g1, m_loc=m_loc, m_old=m_old, m_t=m_t, mrow=mrow, qh=qh, kw=kh * w_c,
            d_w=jnp.exp(log_d - m_t), inter_w=jnp.exp(log_inter - m_t),
            qb=qh.astype(BF16), kb=kh.astype(BF16), vb=mv_ref[z][rows2[z], cols].astype(BF16),
            c_old=c_ref[z, h], n_old=n_ref[z, h])

    g_sc = {zh: _dot_nt(qm[zh], k_dec[zh[0]]) for zh in zh_g}
    g_inter = {zh: _dot_nt(qm[zh], s_old[zh].astype(BF16)) for zh in zh_g}
    g_upd = {zh: _dot_tn(gv[zh], km[zh]) for zh in zh_g}
    m_qk = {zh: _dot_nt(ml[zh]["qb"], ml[zh]["kb"]) for zh in zh_m}
    m_inter = {zh: _dot(ml[zh]["qb"], ml[zh]["c_old"].astype(BF16)) for zh in zh_m}
    m_upd = {zh: _dot_tn(ml[zh]["kw"].astype(BF16), ml[zh]["vb"]) for zh in zh_m}

    g_p = {zh: jnp.where(keep[zh[0]], g_sc[zh], 0.0).astype(BF16) for zh in zh_g}
    m_s = {zh: m_qk[zh] * ml[zh]["d_w"] for zh in zh_m}
    g_intra = {zh: _dot(g_p[zh], gv[zh]) for zh in zh_g}
    m_intra = {zh: _dot(m_s[zh].astype(BF16), ml[zh]["vb"]) for zh in zh_m}

    for z, h in zh_g:
        out_refs[z][rows2[z], h * GLA_DV:(h + 1) * GLA_DV] = g_intra[z, h] + g_inter[z, h]
        s_ref[z, h] = decay[z] * s_old[z, h] + g_upd[z, h]
    for z, h in zh_m:
        e = ml[z, h]
        numer = m_intra[z, h] + e["inter_w"] * m_inter[z, h]
        denom = (jnp.sum(m_s[z, h], axis=1, keepdims=True)
                 + e["inter_w"] * jnp.sum(e["qh"] * e["n_old"], axis=1, keepdims=True))
        col = GLA_H * GLA_DV + h * ML_DH
        out_refs[z][rows2[z], col:col + ML_DH] = numer / jnp.maximum(jnp.abs(denom), jnp.exp(-e["m_t"]))
        m_new = jnp.maximum(e["g1"] + e["m_old"], e["m_loc"])
        a_sc = jnp.exp(e["g1"] + e["m_old"] - m_new)
        w_sc = jnp.exp(e["m_loc"] - m_new)
        c_ref[z, h] = a_sc * e["c_old"] + w_sc * m_upd[z, h]
        n_ref[z, h] = a_sc * e["n_old"] + w_sc * jnp.sum(e["kw"], axis=0, keepdims=True)
        m_ref[e["mrow"]:e["mrow"] + 1, :] = jnp.broadcast_to(m_new, (1, 128))


def _scan_kernel(*args):
    fwd, bwd = args[0:7], args[7:14]
    aup_ref, abias_ref, mgb_ref, of_ref, ob_ref, s_ref, c_ref, n_ref, m_ref = args[14:]

    @pl.when(pl.program_id(1) == 0)
    def _():
        s_ref[...] = jnp.zeros_like(s_ref)
        c_ref[...] = jnp.zeros_like(c_ref)
        n_ref[...] = jnp.zeros_like(n_ref)
        m_ref[...] = jnp.zeros_like(m_ref)

    ti = lax.broadcasted_iota(jnp.int32, (CHUNK, CHUNK), 0)
    si = lax.broadcasted_iota(jnp.int32, (CHUNK, CHUNK), 1)
    keep = [si <= ti, si >= ti]
    tri = [x.astype(BF16) for x in keep]
    tri_t = [(ti <= si).astype(BF16), (ti >= si).astype(BF16)]
    lane = lax.broadcasted_iota(jnp.int32, (CHUNK, GLA_H * GLA_DK), 1)
    in_head = [(lane // GLA_DK) == h for h in range(GLA_H)]
    n_chunks = SB // CHUNK

    def body(ci, carry):
        rows2 = (pl.ds(pl.multiple_of(ci * CHUNK, CHUNK), CHUNK),
                 pl.ds(pl.multiple_of((n_chunks - 1 - ci) * CHUNK, CHUNK), CHUNK))
        _scan_pair(rows2, (fwd, bwd), (of_ref, ob_ref), aup_ref, abias_ref, mgb_ref, s_ref, c_ref, n_ref, m_ref,
                   (keep, tri, tri_t, in_head))
        return carry

    lax.fori_loop(0, n_chunks, body, 0)


def _scan(Z, aup, abias, mgb, B, T, LC):
    R = Z.shape[0]
    assert LC == SB and T % SB == 0
    nb = T // SB
    ctx0 = B * nb
    fmap = lambda b, s: jnp.where(s == 0, ctx0 + b, b * nb + s - 1)
    bmap = lambda b, s: jnp.where(s == 0, ctx0 + b, b * nb + nb - s)

    def specs(rowmap):
        col = lambda width, blk: pl.BlockSpec((SB, width), lambda b, s: (rowmap(b, s), blk))
        return [col(256, 0), col(256, 1), col(512, 1), col(128, SMALL_COL // 128),
                col(512, 3), col(512, 4), col(512, 5)]

    out_spec = lambda rowmap: pl.BlockSpec((SB, D), lambda b, s: (rowmap(b, s), 0))
    return pl.pallas_call(
        _scan_kernel,
        grid=(B, nb + 1),
        in_specs=specs(fmap) + specs(bmap) + [
            pl.BlockSpec((2, 128, 256), lambda b, s: (0, 0, 0)),
            pl.BlockSpec((2, 1, 256), lambda b, s: (0, 0, 0)),
            pl.BlockSpec((1, 128), lambda b, s: (0, 0)),
        ],
        out_specs=[out_spec(fmap), out_spec(bmap)],
        out_shape=[jax.ShapeDtypeStruct((R, D), F32)] * 2,
        scratch_shapes=[
            pltpu.VMEM((2, GLA_H, GLA_DV, GLA_H * GLA_DK), F32),
            pltpu.VMEM((2, ML_H, ML_DH, ML_DH), F32),
            pltpu.VMEM((2, ML_H, 1, ML_DH), F32),
            pltpu.VMEM((2 * ML_H, 128), F32),
        ],
        compiler_params=_params(2),
        name="gla_mlstm_scan",
    )(*([Z] * 14), aup, abias, mgb)


def _even_out_kernel(of_ref, ob_ref, gg_ref, mo_ref, h_ref, gate_ref, gnw_ref, mnw_ref, w_ref, o_ref, y_ref):
    for hh in range(GLA_H + ML_H):
        cols = slice(hh * 128, (hh + 1) * 128)
        x = of_ref[:, cols] + ob_ref[:, cols]
        ms = jnp.mean(x * x, axis=-1, keepdims=True)
        if hh < GLA_H:
            gt = gg_ref[:, cols]
            y = x * lax.rsqrt(ms + EPS) * gnw_ref[...] * (gt * _sigmoid(gt))
        else:
            gt = mo_ref[:, (hh - GLA_H) * 128:(hh - GLA_H + 1) * 128]
            y = x * lax.rsqrt(ms + EPS) * mnw_ref[...] * _sigmoid(gt)
        y_ref[:, cols] = y.astype(BF16)
    o_ref[...] = h_ref[...] + gate_ref[0] * _dot(y_ref[...], w_ref[...])


def _even_out(of, ob, Z, H, gate, gnw, mnw, w, n_lat_tiles, tiles_per_seq, ctx_row):
    R = H.shape[0]
    rowmap = _mod_row_map(n_lat_tiles, tiles_per_seq, ctx_row)
    return pl.pallas_call(
        _even_out_kernel,
        grid=(R // TM,),
        in_specs=[
            pl.BlockSpec((TM, D), lambda i: (i, 0)),
            pl.BlockSpec((TM, D), lambda i: (i, 0)),
            pl.BlockSpec((TM, 512), lambda i: (i, 2)),
            pl.BlockSpec((TM, 512), lambda i: (i, 6)),
            pl.BlockSpec((TM, D), lambda i: (i, 0)),
            pl.BlockSpec((1, 1, D), rowmap),
            pl.BlockSpec((1, 128), lambda i: (0, 0)),
            pl.BlockSpec((1, 128), lambda i: (0, 0)),
            _resident((D, D)),
        ],
        out_specs=pl.BlockSpec((TM, D), lambda i: (i, 0)),
        out_shape=jax.ShapeDtypeStruct((R, D), F32),
        scratch_shapes=[pltpu.VMEM((TM, D), BF16)],
        input_output_aliases={4: 0},
        compiler_params=_params(),
        name="even_out_proj",
    )(of, ob, Z, Z, H, gate, gnw, mnw, w)


def _na_proj_kernel(h_ref, mult_ref, shift_ref, w_ref, z_ref):
    u = _norm_mod(h_ref[...], mult_ref[0], shift_ref[0]).astype(BF16)
    for j in range(3 * D // 256):
        z = _dot(u, w_ref[:, j * 256:(j + 1) * 256])
        if j < D // 256:
            z = z * (NA_DH ** -0.5)
        z_ref[:, j * 256:(j + 1) * 256] = z.astype(BF16)


def _na_proj(H, mult, shift, w, n_lat_tiles, tiles_per_seq, ctx_row):
    R = H.shape[0]
    rowmap = _mod_row_map(n_lat_tiles, tiles_per_seq, ctx_row)
    return pl.pallas_call(
        _na_proj_kernel,
        grid=(R // TM,),
        in_specs=[
            pl.BlockSpec((TM, D), lambda i: (i, 0)),
            pl.BlockSpec((1, 1, D), rowmap),
            pl.BlockSpec((1, 1, D), rowmap),
            _resident((D, 3 * D)),
        ],
        out_specs=pl.BlockSpec((TM, 3 * D), lambda i: (i, 0)),
        out_shape=jax.ShapeDtypeStruct((R, 3 * D), BF16),
        compiler_params=_params(),
        name="na_qkv_proj",
    )(H, mult, shift, w)


NA_ROWS_PER_STEP = 4


def _stack_heads(q):
    lane = lax.broadcasted_iota(jnp.int32, q.shape, 1)
    zero = jnp.zeros_like(q)
    return jnp.concatenate([jnp.where(lane < NA_DH, q, zero), jnp.where(lane >= NA_DH, q, zero)], axis=0)


def _softmax_pv(s_list, v_list):
    m = functools.reduce(jnp.maximum, [jnp.max(s, axis=1, keepdims=True) for s in s_list])
    p_list = [jnp.exp(s - m) for s in s_list]
    l = functools.reduce(jnp.add, [jnp.sum(p, axis=1, keepdims=True) for p in p_list])
    o = functools.reduce(jnp.add, [_dot(p.astype(BF16), v) for p, v in zip(p_list, v_list)])
    return o / l


def _unstack_heads(o):
    n = o.shape[0] // 2
    lane = lax.broadcasted_iota(jnp.int32, (n, 128), 1)
    return jnp.where(lane < NA_DH, o[:n], o[n:])


def _na_kernel(q_ref, k_ref, v_ref, kc_ref, vc_ref, qc_ref, tb_ref, o_ref, oc_ref, *, rows):
    kc = kc_ref[...]
    vc = vc_ref[...]
    win = NA_WIN_H * GRID_W

    def row_body(i, carry):
        rs = [i * NA_ROWS_PER_STEP + u for u in range(NA_ROWS_PER_STEP)]
        kr0 = [jnp.clip(r - NA_WIN_H // 2, 0, rows - NA_WIN_H) for r in rs]
        qrows = [pl.ds(pl.multiple_of(r * GRID_W, GRID_W), GRID_W) for r in rs]
        krows = [pl.ds(pl.multiple_of(k0 * GRID_W, GRID_W), win) for k0 in kr0]
        q2 = [_stack_heads(q_ref[qr, :]) for qr in qrows]
        s_w = [_dot_nt(q, k_ref[kr, :]) for q, kr in zip(q2, krows)]
        s_c = [_dot_nt(q, kc) for q in q2]
        s_w = [s + tb_ref[0, r - k0] for s, r, k0 in zip(s_w, rs, kr0)]
        m = [jnp.maximum(jnp.max(a, axis=1, keepdims=True), jnp.max(b, axis=1, keepdims=True))
             for a, b in zip(s_w, s_c)]
        p_w = [jnp.exp(s - mm) for s, mm in zip(s_w, m)]
        p_c = [jnp.exp(s - mm) for s, mm in zip(s_c, m)]
        l = [jnp.sum(a, axis=1, keepdims=True) + jnp.sum(b, axis=1, keepdims=True) for a, b in zip(p_w, p_c)]
        o = [_dot(a.astype(BF16), v_ref[kr, :]) + _dot(b.astype(BF16), vc) for a, b, kr in zip(p_w, p_c, krows)]
        for oo, ll, qr in zip(o, l, qrows):
            o_ref[qr, :] = _unstack_heads(oo / ll).astype(o_ref.dtype)
        return carry

    lax.fori_loop(0, rows // NA_ROWS_PER_STEP, row_body, 0)

    oc = _softmax_pv([_dot_nt(_stack_heads(qc_ref[...]), kc)], [vc])
    oc_ref[...] = _unstack_heads(oc).astype(oc_ref.dtype)


def _na_attention(Zq, tb, B, T, LC):
    rows = T // GRID_W
    assert rows % NA_ROWS_PER_STEP == 0 and rows >= NA_WIN_H
    n_pairs = NA_H // 2
    ctx_blk0 = B * T // LC
    lat = lambda part: pl.BlockSpec((T, 128), lambda b, p: (b, part * n_pairs + p))
    ctx = lambda part: pl.BlockSpec((LC, 128), lambda b, p: (ctx_blk0 + b, part * n_pairs + p))
    return pl.pallas_call(
        functools.partial(_na_kernel, rows=rows),
        grid=(B, n_pairs),
        in_specs=[lat(0), lat(1), lat(2), ctx(1), ctx(2), ctx(0),
                  pl.BlockSpec((1, NA_WIN_H, 2 * GRID_W, NA_WIN_H * GRID_W), lambda b, p: (p, 0, 0, 0))],
        out_specs=[pl.BlockSpec((T, 128), lambda b, p: (b, p)),
                   pl.BlockSpec((LC, 128), lambda b, p: (b, p))],
        out_shape=[jax.ShapeDtypeStruct((B * T, D), BF16), jax.ShapeDtypeStruct((B * LC, D), BF16)],
        compiler_params=_params(2),
        name="na_attention",
    )(Zq, Zq, Zq, Zq, Zq, Zq, tb)


def _na_bias_tables(rpb):
    n_dr = 2 * NA_WIN_H - 1
    pad = GRID_W - NA_WIN_W
    w = jnp.pad(rpb.astype(F32), ((0, 0), (0, 0), (pad, pad + 1)))
    skew = jnp.broadcast_to(w[:, :, None, :], (NA_H, n_dr, GRID_W, 128)).reshape(NA_H, n_dr, GRID_W * 128)
    skew = skew[:, :, :GRID_W * 127].reshape(NA_H, n_dr, GRID_W, 127)
    toep = skew[:, :, :, GRID_W - 1:]
    c = np.arange(GRID_W)[:, None]
    kc = np.arange(GRID_W)[None, :]
    start = np.clip(c - NA_WIN_W // 2, 0, GRID_W - NA_WIN_W)
    inside = (kc >= start) & (kc < start + NA_WIN_W)
    toep = jnp.where(inside[None, None], toep, -1e30)
    per_shift = jnp.stack([toep[:, NA_WIN_H - 1 - s:2 * NA_WIN_H - 1 - s] for s in range(NA_WIN_H)], axis=1)
    tb = per_shift.transpose(0, 1, 3, 2, 4).reshape(NA_H // 2, 2, NA_WIN_H, GRID_W, NA_WIN_H * GRID_W)
    return tb.transpose(0, 2, 1, 3, 4).reshape(NA_H // 2, NA_WIN_H, 2 * GRID_W, NA_WIN_H * GRID_W)


def _na_out_kernel(y_ref, h_ref, gate_ref, w_ref, o_ref):
    o_ref[...] = h_ref[...] + gate_ref[0] * _dot(y_ref[...], w_ref[...])


def _na_out(Y, H, gate, w, n_lat_tiles, tiles_per_seq, ctx_row):
    R = H.shape[0]
    rowmap = _mod_row_map(n_lat_tiles, tiles_per_seq, ctx_row)
    return pl.pallas_call(
        _na_out_kernel,
        grid=(R // TM,),
        in_specs=[
            pl.BlockSpec((TM, D), lambda i: (i, 0)),
            pl.BlockSpec((TM, D), lambda i: (i, 0)),
            pl.BlockSpec((1, 1, D), rowmap),
            _resident((D, D)),
        ],
        out_specs=pl.BlockSpec((TM, D), lambda i: (i, 0)),
        out_shape=jax.ShapeDtypeStruct((R, D), F32),
        input_output_aliases={1: 0},
        compiler_params=_params(),
        name="na_out_proj",
    )(Y, H, gate, w)


def _ffn_kernel(hp_ref, h_ref, hn_ref, mult_ref, shift_ref, gate_ref, wup_ref, cw_ref, cb_ref, wd_ref, fnw_ref,
                o_ref, u_ref, act_ref, *, final, n_lat_rows, T, LC):
    x = h_ref[...]
    mult = mult_ref[0]
    shift = shift_ref[0]
    u_ref[0:HALO, :] = _norm_mod(hp_ref[...], mult, shift).astype(BF16)
    u_ref[HALO:HALO + TM, :] = _norm_mod(x, mult, shift).astype(BF16)
    u_ref[HALO + TM:, :] = _norm_mod(hn_ref[...], mult, shift).astype(BF16)

    r = pl.program_id(0) * TM + lax.broadcasted_iota(jnp.int32, (TM, 1), 0)
    is_lat = r < n_lat_rows
    pos = jnp.where(is_lat, r & (T - 1), r & (LC - 1))
    first = pos == 0
    last = pos == jnp.where(is_lat, T - 1, LC - 1)

    n_ext = TM + 2 * HALO
    for c in range(D_FF // FC):
        cols = slice(c * FC, (c + 1) * FC)
        a = _dot(u_ref[...], wup_ref[:, cols])
        a_prev = pltpu.roll(a, 1, 0)[HALO:HALO + TM]
        a_next = pltpu.roll(a, n_ext - 1, 0)[HALO:HALO + TM]
        a_mid = a[HALO:HALO + TM]
        conv = (cw_ref[0:1, cols] * jnp.where(first, 0.0, a_prev) + cw_ref[1:2, cols] * a_mid
                + cw_ref[2:3, cols] * jnp.where(last, 0.0, a_next) + cb_ref[:, cols])
        gt = _dot(u_ref[HALO:HALO + TM, :], wup_ref[:, D_FF + c * FC:D_FF + (c + 1) * FC])
        act_ref[:, cols] = (jax.nn.gelu(conv, approximate=True) * gt).astype(BF16)

    y = x + gate_ref[0] * _dot(act_ref[...], wd_ref[...])
    if final:
        y = _norm_mod(y, fnw_ref[...], 0.0)
    o_ref[...] = y


def _ffn(H, mult, shift, gate, wup, cw, cb, wd, fnw, *, final, n_rows_out, n_lat_tiles, tiles_per_seq,
         ctx_row, n_lat_rows, T, LC):
    R = H.shape[0]
    rowmap = _mod_row_map(n_lat_tiles, tiles_per_seq, ctx_row)
    per = TM // HALO
    last_halo_blk = R // HALO - 1
    kern = functools.partial(_ffn_kernel, final=final, n_lat_rows=n_lat_rows, T=T, LC=LC)
    return pl.pallas_call(
        kern,
        grid=(n_rows_out // TM,),
        in_specs=[
            pl.BlockSpec((HALO, D), lambda i: (jnp.maximum(i * per - 1, 0), 0)),
            pl.BlockSpec((TM, D), lambda i: (i, 0)),
            pl.BlockSpec((HALO, D), lambda i: (jnp.minimum((i + 1) * per, last_halo_blk), 0)),
            pl.BlockSpec((1, 1, D), rowmap),
            pl.BlockSpec((1, 1, D), rowmap),
            pl.BlockSpec((1, 1, D), rowmap),
            _resident((D, 2 * D_FF)),
            pl.BlockSpec((3, D_FF), lambda i: (0, 0)),
            pl.BlockSpec((1, D_FF), lambda i: (0, 0)),
            _resident((D_FF, D)),
            pl.BlockSpec((1, D), lambda i: (0, 0)),
        ],
        out_specs=pl.BlockSpec((TM, D), lambda i: (i, 0)),
        out_shape=jax.ShapeDtypeStruct((n_rows_out, D), F32),
        scratch_shapes=[pltpu.VMEM((TM + 2 * HALO, D), BF16), pltpu.VMEM((TM, D_FF), BF16)],
        compiler_params=_params(),
        name="conv_ffn_final" if final else "conv_ffn",
    )(H, H, H, mult, shift, gate, wup, cw, cb, wd, fnw)


def _rope_tables(T, head_dim):
    pos = jnp.arange(T, dtype=jnp.int32)
    row = (pos // GRID_W).astype(F32)
    col = (pos % GRID_W).astype(F32)
    n_freq = head_dim // 4
    inv_freq = ROPE_BASE ** (-jnp.arange(n_freq, dtype=F32) / n_freq)
    ang = jnp.concatenate([row[:, None] * inv_freq, col[:, None] * inv_freq], axis=-1)
    cos, sin = jnp.cos(ang), jnp.sin(ang)
    reps = 128 // head_dim
    c = jnp.tile(jnp.concatenate([cos, cos], axis=-1), (1, reps))
    s = jnp.tile(jnp.concatenate([-sin, sin], axis=-1), (1, reps))
    c = jnp.concatenate([c, jnp.ones((TM, 128), F32)], axis=0)
    s = jnp.concatenate([s, jnp.zeros((TM, 128), F32)], axis=0)
    return c, s


def kernel(x, c, ctx, c_ctx, mod_w, mod_b, norm_mix_w, norm_ffn_w, ev_w_in, ev_w_out, gla_a_up, gla_a_bias,
           gla_norm_w, ml_gate_bias, ml_norm_w, na_w_qkv, na_w_out, na_rpb, ffn_w_up, ffn_conv_w, ffn_conv_b,
           ffn_w_down, final_norm_w):
    B, T, _ = x.shape
    LC = ctx.shape[1]
    assert x.shape[2] == D and T % TM == 0 and (B * LC) % TM == 0 and B < N_MOD_ROWS
    assert T & (T - 1) == 0 and LC & (LC - 1) == 0 and T % GRID_W == 0
    n_lat_rows = B * T
    n_lat_tiles = n_lat_rows // TM
    tiles_per_seq = T // TM
    tile_kw = dict(n_lat_tiles=n_lat_tiles, tiles_per_seq=tiles_per_seq, ctx_row=B)

    H = jnp.concatenate([x.reshape(B * T, D), ctx.reshape(B * LC, D)], axis=0)
    R = H.shape[0]

    cvec = jnp.zeros((N_MOD_ROWS, D), F32).at[:B].set(c).at[B].set(c_ctx)
    mods = _modulation(cvec, mod_w, mod_b)
    ropes = _rope_tables(T, GLA_DK) + _rope_tables(T, ML_DH)

    out = None
    for layer in range(DEPTH):
        j = layer // 2
        last = layer == DEPTH - 1
        sh1, sc1, g1, sh2, sc2, g2 = [m.reshape(N_MOD_ROWS, 1, D) for m in jnp.split(mods[layer], 6, axis=-1)]
        mult1 = norm_mix_w[layer] * (1.0 + sc1)
        mult2 = norm_ffn_w[layer] * (1.0 + sc2)
        if layer % 2 == 0:
            w = ev_w_in[j]
            w_re = jnp.concatenate(
                [w[:, 0:1536], w[:, 1568:3616], w[:, 1536:1568], w[:, 3616:3632],
                 jnp.zeros((D, EVEN_N - 3632), F32)], axis=1).astype(BF16)
            Z = _even_proj(H, mult1, sh1, w_re, ropes, **tile_kw)
            aup = jnp.zeros((2, 128, GLA_H * GLA_DK), F32)
            aup = aup.at[0, 0:GLA_RANK].set(gla_a_up[j, 0]).at[1, GLA_RANK:2 * GLA_RANK].set(gla_a_up[j, 1])
            mgb = jnp.zeros((1, 128), F32).at[0, 2 * GLA_RANK:2 * GLA_RANK + 4 * ML_H].set(ml_gate_bias[j].reshape(-1))
            of, ob = _scan(Z, aup.astype(BF16), gla_a_bias[j].reshape(2, 1, -1), mgb, B, T, LC)
            H = _even_out(of, ob, Z, H, g1, gla_norm_w[j].reshape(1, -1), ml_norm_w[j].reshape(1, -1),
                          ev_w_out[j].astype(BF16), **tile_kw)
        else:
            Zq = _na_proj(H, mult1, sh1, na_w_qkv[j].astype(BF16), **tile_kw)
            o_lat, o_ctx = _na_attention(Zq, _na_bias_tables(na_rpb[j]), B, T, LC)
            H = _na_out(jnp.concatenate([o_lat, o_ctx], axis=0), H, g1, na_w_out[j].astype(BF16), **tile_kw)
        res = _ffn(H, mult2, sh2, g2, ffn_w_up[layer].astype(BF16), ffn_conv_w[layer],
                   ffn_conv_b[layer].reshape(1, -1), ffn_w_down[layer].astype(BF16), final_norm_w.reshape(1, -1),
                   final=last, n_rows_out=n_lat_rows if last else R, n_lat_rows=n_lat_rows, T=T, LC=LC, **tile_kw)
        if last:
            out = res
        else:
            H = res
    return out.reshape(B, T, D)
```

```python
import functools

import numpy as np
import jax
import jax.numpy as jnp
from jax import lax
from jax.experimental import pallas as pl
from jax.experimental.pallas import tpu as pltpu

F32 = jnp.float32
BF16 = jnp.bfloat16

D = 1024
GRID_W = 64
EPS = 1e-6
ROPE_BASE = 10000.0
CHUNK = 64
GLA_H, GLA_DK, GLA_DV, GLA_RANK, GLA_TAU = 4, 64, 128, 16, 16.0
ML_H, ML_DH = 4, 128
NA_H, NA_DH, NA_WIN_H, NA_WIN_W = 16, 64, 8, 16
D_FF = 2816
DEPTH = 4
N_MOD_ROWS = 16

TM = 512
SB = 256
HALO = 16
FC = 256
EVEN_N = 3712
SMALL_COL = 3584
VMEM_LIMIT = 58 * 1024 * 1024


def _dot(a, b):
    return jnp.dot(a, b, preferred_element_type=F32)


def _dot_nt(a, b):
    return lax.dot_general(a, b, (((1,), (1,)), ((), ())), preferred_element_type=F32)


def _dot_tn(a, b):
    return lax.dot_general(a, b, (((0,), (0,)), ((), ())), preferred_element_type=F32)


def _split_bf16(x):
    hi = x.astype(BF16)
    lo = (x - hi.astype(F32)).astype(BF16)
    return hi, lo


def _log_sigmoid(x):
    return jnp.minimum(x, 0.0) - jnp.log(1.0 + jnp.exp(-jnp.abs(x)))


def _sigmoid(x):
    return 1.0 / (1.0 + jnp.exp(-x))


def _params(n_axes=1):
    return pltpu.CompilerParams(dimension_semantics=("arbitrary",) * n_axes, vmem_limit_bytes=VMEM_LIMIT)


def _resident(shape):
    nd = len(shape)
    return pl.BlockSpec(shape, lambda *_: (0,) * nd, pipeline_mode=pl.Buffered(1))


def _mod_kernel(cv_ref, w_ref, b_ref, o_ref):
    a = cv_ref[...]
    a = a * _sigmoid(a)
    w = w_ref[0]
    a_hi, a_lo = _split_bf16(a)
    w_hi, w_lo = _split_bf16(w)
    o_ref[0] = _dot(a_hi, w_hi) + _dot(a_hi, w_lo) + _dot(a_lo, w_hi) + b_ref[0]


def _modulation(cvec, mod_w, mod_b):
    tn = 1536
    return pl.pallas_call(
        _mod_kernel,
        grid=(DEPTH, 6 * D // tn),
        in_specs=[
            pl.BlockSpec((N_MOD_ROWS, D), lambda l, j: (0, 0)),
            pl.BlockSpec((1, D, tn), lambda l, j: (l, 0, j)),
            pl.BlockSpec((1, 1, tn), lambda l, j: (l, 0, j)),
        ],
        out_specs=pl.BlockSpec((1, N_MOD_ROWS, tn), lambda l, j: (l, 0, j)),
        out_shape=jax.ShapeDtypeStruct((DEPTH, N_MOD_ROWS, 6 * D), F32),
        compiler_params=_params(2),
        name="adaln_table",
    )(cvec, mod_w, mod_b.reshape(DEPTH, 1, 6 * D))


def _norm_mod(x, mult, shift):
    ms = jnp.mean(x * x, axis=-1, keepdims=True)
    return x * lax.rsqrt(ms + EPS) * mult + shift


def _mod_row_map(n_lat_tiles, tiles_per_seq, ctx_row):
    def index_map(i):
        return (jnp.where(i < n_lat_tiles, i // tiles_per_seq, ctx_row), 0, 0)
    return index_map


def _rope_gla(x, c, s):
    lane = lax.broadcasted_iota(jnp.int32, x.shape, 1)
    swapped = jnp.where((lane % GLA_DK) < GLA_DK // 2, pltpu.roll(x, 128 - GLA_DK // 2, 1),
                        pltpu.roll(x, GLA_DK // 2, 1))
    return x * c + swapped * s


def _rope_ml(x, c, s):
    return x * c + pltpu.roll(x, ML_DH // 2, 1) * s


def _even_proj_kernel(h_ref, mult_ref, shift_ref, w_ref, cg_ref, sg_ref, cm_ref, sm_ref, z_ref):
    u = _norm_mod(h_ref[...], mult_ref[0], shift_ref[0]).astype(BF16)
    q_scale = {0: GLA_DK ** -0.5, 1: 1.0, 6: 1.0, 7: 1.0, 8: ML_DH ** -0.5, 9: ML_DH ** -0.5}
    for j in range(SMALL_COL // 256):
        z = _dot(u, w_ref[:, j * 256:(j + 1) * 256])
        if j in q_scale:
            rope = _rope_gla if j < 2 else _rope_ml
            c_ref, s_ref = (cg_ref, sg_ref) if j < 2 else (cm_ref, sm_ref)
            for half in range(2):
                zz = rope(z[:, half * 128:(half + 1) * 128], c_ref[...], s_ref[...]) * q_scale[j]
                z_ref[:, j * 256 + half * 128:j * 256 + (half + 1) * 128] = zz
        else:
            z_ref[:, j * 256:(j + 1) * 256] = z
    z_ref[:, SMALL_COL:EVEN_N] = _dot(u, w_ref[:, SMALL_COL:EVEN_N])


def _even_proj(H, mult, shift, w, ropes, n_lat_tiles, tiles_per_seq, ctx_row):
    R = H.shape[0]
    rowmap = _mod_row_map(n_lat_tiles, tiles_per_seq, ctx_row)
    ropemap = lambda i: (jnp.where(i < n_lat_tiles, i % tiles_per_seq, tiles_per_seq), 0)
    return pl.pallas_call(
        _even_proj_kernel,
        grid=(R // TM,),
        in_specs=[
            pl.BlockSpec((TM, D), lambda i: (i, 0)),
            pl.BlockSpec((1, 1, D), rowmap),
            pl.BlockSpec((1, 1, D), rowmap),
            _resident((D, EVEN_N)),
        ] + [pl.BlockSpec((TM, 128), ropemap)] * 4,
        out_specs=pl.BlockSpec((TM, EVEN_N), lambda i: (i, 0)),
        out_shape=jax.ShapeDtypeStruct((R, EVEN_N), F32),
        compiler_params=_params(),
        name="even_in_proj",
    )(H, mult, shift, w, *ropes)


def _chunk_cummax(x, z, tmod):
    n = x.shape[0]
    k = 1
    while k < CHUNK:
        if z == 0:
            shifted, valid = pltpu.roll(x, k, 0), tmod >= k
        else:
            shifted, valid = pltpu.roll(x, n - k, 0), tmod < CHUNK - k
        x = jnp.maximum(x, jnp.where(valid, shifted, -jnp.inf))
        k *= 2
    return x


def _scan_kernel(*args):
    in_refs = (args[0:7], args[7:14])
    aup_ref, abias_ref, mgb_ref, of_ref, ob_ref, s_ref, c_ref, m_ref = args[14:]
    out_refs = (of_ref, ob_ref)
    dirs = (0, 1)
    n_chunks = SB // CHUNK
    order = (tuple(range(n_chunks)), tuple(reversed(range(n_chunks))))
    edge = lambda z, c: c * CHUNK + (CHUNK - 1 if z == 0 else 0)
    rows_of = lambda c: slice(c * CHUNK, (c + 1) * CHUNK)

    @pl.when(pl.program_id(1) == 0)
    def _():
        s_ref[...] = jnp.zeros_like(s_ref)
        c_ref[...] = jnp.zeros_like(c_ref)
        m_ref[...] = jnp.zeros_like(m_ref)

    ti = lax.broadcasted_iota(jnp.int32, (SB, SB), 0)
    si = lax.broadcasted_iota(jnp.int32, (SB, SB), 1)
    same = (ti // CHUNK) == (si // CHUNK)
    mask = [same & (si <= ti), same & (si >= ti)]
    tri = [x.astype(BF16) for x in mask]
    same_b = same.astype(BF16)
    tmod = lax.broadcasted_iota(jnp.int32, (SB, 128), 0) % CHUNK
    lane256 = lax.broadcasted_iota(jnp.int32, (SB, GLA_H * GLA_DK), 1)
    in_head = [(lane256 // GLA_DK) == h for h in range(GLA_H)]
    lane128 = lax.broadcasted_iota(jnp.int32, (1, 128), 1)
    gate_col = lambda z, h: 2 * GLA_RANK + z * 2 * ML_H + h
    ones_b = jnp.ones((SB, ML_DH), BF16)

    P = []
    for z in dirs:
        gq_ref, gk_ref, gv_ref, sm_ref, mq_ref, mk_ref, mv_ref = in_refs[z]
        small = sm_ref[...]
        a_pre = _dot(small.astype(BF16), aup_ref[z]) + abias_ref[z]
        la_hi, la_lo = _split_bf16(_log_sigmoid(a_pre) * (1.0 / GLA_TAU))
        b = _dot(tri[z], la_hi) + _dot(tri[z], la_lo)
        gb = _dot(same_b, la_hi) + _dot(same_b, la_lo)
        q = gq_ref[...]
        k = gk_ref[...]
        gates = small + mgb_ref[...]
        lf_hi, lf_lo = _split_bf16(_log_sigmoid(gates))
        bc = pltpu.roll(_dot(tri[z], lf_hi) + _dot(tri[z], lf_lo), 128 - ML_H, 1)
        gc = pltpu.roll(_dot(same_b, lf_hi) + _dot(same_b, lf_lo), 128 - ML_H, 1)
        w = gates - bc
        cm = _chunk_cummax(w, z, tmod)
        P.append(dict(
            q_dec=(q * jnp.exp(b)).astype(BF16), k_dec=(k * jnp.exp(-b)).astype(BF16),
            k_end=(k * jnp.exp(gb - b)).astype(BF16), gb=gb, bc=bc, gc=gc, w=w, cm=cm, w_t=w.T))

    G, M = {}, {}
    for z in dirs:
        gq_ref, gk_ref, gv_ref, sm_ref, mq_ref, mk_ref, mv_ref = in_refs[z]
        p = P[z]
        for h in range(GLA_H):
            qm = jnp.where(in_head[h], p["q_dec"], jnp.zeros_like(p["q_dec"]))
            km = jnp.where(in_head[h], p["k_end"], jnp.zeros_like(p["k_end"]))
            v = gv_ref[:, h * GLA_DV:(h + 1) * GLA_DV].astype(BF16)
            G[z, h] = dict(qm=qm, km=km, v=v, sc=_dot_nt(qm, p["k_dec"]))
        for h in range(ML_H):
            fi = gate_col(z, h)
            cols = slice(h * ML_DH, (h + 1) * ML_DH)
            d = jnp.exp(jnp.where(mask[z], p["w_t"][fi:fi + 1, :] - p["cm"][:, fi:fi + 1], -jnp.inf))
            qb = mq_ref[:, cols].astype(BF16)
            va = jnp.concatenate([mv_ref[:, cols].astype(BF16), ones_b], axis=1)
            M[z, h] = dict(qb=qb, va=va, d=d, qk=_dot_nt(qb, mk_ref[:, cols].astype(BF16)))
    for z in dirs:
        for h in range(GLA_H):
            e = G[z, h]
            e["intra"] = _dot(jnp.where(mask[z], e["sc"], 0.0).astype(BF16), e["v"])
        for h in range(ML_H):
            e = M[z, h]
            e["intra"] = _dot((e["qk"] * e["d"]).astype(BF16), e["va"])
    for z in dirs:
        mk_ref = in_refs[z][5]
        p = P[z]
        for c in range(n_chunks):
            r = rows_of(c)
            wmax = p["cm"][edge(z, c):edge(z, c) + 1, :]
            w_end = jnp.exp(p["w"][r] - wmax)
            p["m_loc", c] = wmax + p["gc"][c * CHUNK:c * CHUNK + 1, :]
            for h in range(GLA_H):
                e = G[z, h]
                e["kv", c] = _dot_tn(e["v"][r], e["km"][r])
            for h in range(ML_H):
                e = M[z, h]
                fi = gate_col(z, h)
                kw = (mk_ref[r, h * ML_DH:(h + 1) * ML_DH] * w_end[:, fi:fi + 1]).astype(BF16)
                e["cl", c] = _dot_tn(kw, e["va"][r])

    S = {(z, h): s_ref[z, h] for z in dirs for h in range(GLA_H)}
    C = {(z, h): c_ref[z, h] for z in dirs for h in range(ML_H)}
    m_old = [m_ref[z] for z in dirs]
    head_lanes = [functools.reduce(jnp.logical_or, [lane128 == gate_col(z, h) for h in range(ML_H)]) for z in dirs]
    for step in range(n_chunks):
        g_inter, m_inter = {}, {}
        for z in dirs:
            r = rows_of(order[z][step])
            for h in range(GLA_H):
                g_inter[z, h] = _dot_nt(G[z, h]["qm"][r], S[z, h].astype(BF16))
            for h in range(ML_H):
                m_inter[z, h] = _dot(M[z, h]["qb"][r], C[z, h].astype(BF16))
        for z in dirs:
            c = order[z][step]
            r = rows_of(c)
            p = P[z]
            decay = jnp.exp(p["gb"][c * CHUNK:c * CHUNK + 1, :])
            for h in range(GLA_H):
                out_refs[z][r, h * GLA_DV:(h + 1) * GLA_DV] = G[z, h]["intra"][r] + g_inter[z, h]
                S[z, h] = decay * S[z, h] + G[z, h]["kv", c]
            mx = jnp.maximum(m_old[z], p["cm"][r])
            alpha = jnp.exp(p["cm"][r] - mx)
            inter_w = jnp.exp(m_old[z] - mx)
            floor = jnp.exp(-(p["bc"][r] + mx))
            g_c = p["gc"][c * CHUNK:c * CHUNK + 1, :]
            m_new = jnp.where(head_lanes[z], jnp.maximum(g_c + m_old[z], p["m_loc", c]), 0.0)
            a_sc = jnp.exp(g_c + m_old[z] - m_new)
            w_sc = jnp.exp(p["m_loc", c] - m_new)
            for h in range(ML_H):
                fi = gate_col(z, h)
                al, iw = alpha[:, fi:fi + 1], inter_w[:, fi:fi + 1]
                tot = al * M[z, h]["intra"][r] + iw * m_inter[z, h]
                den = jnp.maximum(jnp.abs(tot[:, ML_DH:]), floor[:, fi:fi + 1])
                col = GLA_H * GLA_DV + h * ML_DH
                out_refs[z][r, col:col + ML_DH] = tot[:, :ML_DH] / den
                C[z, h] = a_sc[:, fi:fi + 1] * C[z, h] + w_sc[:, fi:fi + 1] * M[z, h]["cl", c]
            m_old[z] = m_new
    for z in dirs:
        for h in range(GLA_H):
            s_ref[z, h] = S[z, h]
        for h in range(ML_H):
            c_ref[z, h] = C[z, h]
        m_ref[z] = m_old[z]


def _scan(Z, aup, abias, mgb, B, T, LC):
    R = Z.shape[0]
    assert LC == SB and T % SB == 0
    nb = T // SB
    ctx0 = B * nb
    fmap = lambda b, s: jnp.where(s == 0, ctx0 + b, b * nb + s - 1)
    bmap = lambda b, s: jnp.where(s == 0, ctx0 + b, b * nb + nb - s)

    def specs(rowmap):
        col = lambda width, blk: pl.BlockSpec((SB, width), lambda b, s: (rowmap(b, s), blk))
        return [col(256, 0), col(256, 1), col(512, 1), col(128, SMALL_COL // 128),
                col(512, 3), col(512, 4), col(512, 5)]

    out_spec = lambda rowmap: pl.BlockSpec((SB, D), lambda b, s: (rowmap(b, s), 0))
    return pl.pallas_call(
        _scan_kernel,
        grid=(B, nb + 1),
        in_specs=specs(fmap) + specs(bmap) + [
            pl.BlockSpec((2, 128, 256), lambda b, s: (0, 0, 0)),
            pl.BlockSpec((2, 1, 256), lambda b, s: (0, 0, 0)),
            pl.BlockSpec((1, 128), lambda b, s: (0, 0)),
        ],
        out_specs=[out_spec(fmap), out_spec(bmap)],
        out_shape=[jax.ShapeDtypeStruct((R, D), F32)] * 2,
        scratch_shapes=[
            pltpu.VMEM((2, GLA_H, GLA_DV, GLA_H * GLA_DK), F32),
            pltpu.VMEM((2, ML_H, ML_DH, 2 * ML_DH), F32),
            pltpu.VMEM((2, 1, 128), F32),
        ],
        compiler_params=_params(2),
        name="gla_mlstm_scan",
    )(*([Z] * 14), aup, abias, mgb)


def _even_out_kernel(of_ref, ob_ref, gg_ref, mo_ref, h_ref, gate_ref, gnw_ref, mnw_ref, w_ref, o_ref, y_ref):
    for hh in range(GLA_H + ML_H):
        cols = slice(hh * 128, (hh + 1) * 128)
        x = of_ref[:, cols] + ob_ref[:, cols]
        ms = jnp.mean(x * x, axis=-1, keepdims=True)
        if hh < GLA_H:
            gt = gg_ref[:, cols]
            y = x * lax.rsqrt(ms + EPS) * gnw_ref[...] * (gt * _sigmoid(gt))
        else:
            gt = mo_ref[:, (hh - GLA_H) * 128:(hh - GLA_H + 1) * 128]
            y = x * lax.rsqrt(ms + EPS) * mnw_ref[...] * _sigmoid(gt)
        y_ref[:, cols] = y.astype(BF16)
    o_ref[...] = h_ref[...] + gate_ref[0] * _dot(y_ref[...], w_ref[...])


def _even_out(of, ob, Z, H, gate, gnw, mnw, w, n_lat_tiles, tiles_per_seq, ctx_row):
    R = H.shape[0]
    rowmap = _mod_row_map(n_lat_tiles, tiles_per_seq, ctx_row)
    return pl.pallas_call(
        _even_out_kernel,
        grid=(R // TM,),
        in_specs=[
            pl.BlockSpec((TM, D), lambda i: (i, 0)),
            pl.BlockSpec((TM, D), lambda i: (i, 0)),
            pl.BlockSpec((TM, 512), lambda i: (i, 2)),
            pl.BlockSpec((TM, 512), lambda i: (i, 6)),
            pl.BlockSpec((TM, D), lambda i: (i, 0)),
            pl.BlockSpec((1, 1, D), rowmap),
            pl.BlockSpec((1, 128), lambda i: (0, 0)),
            pl.BlockSpec((1, 128), lambda i: (0, 0)),
            _resident((D, D)),
        ],
        out_specs=pl.BlockSpec((TM, D), lambda i: (i, 0)),
        out_shape=jax.ShapeDtypeStruct((R, D), F32),
        scratch_shapes=[pltpu.VMEM((TM, D), BF16)],
        input_output_aliases={4: 0},
        compiler_params=_params(),
        name="even_out_proj",
    )(of, ob, Z, Z, H, gate, gnw, mnw, w)


def _na_proj_kernel(h_ref, mult_ref, shift_ref, w_ref, z_ref):
    u = _norm_mod(h_ref[...], mult_ref[0], shift_ref[0]).astype(BF16)
    for j in range(3 * D // 256):
        z = _dot(u, w_ref[:, j * 256:(j + 1) * 256])
        if j < D // 256:
            z = z * (NA_DH ** -0.5)
        z_ref[:, j * 256:(j + 1) * 256] = z.astype(BF16)


def _na_proj(H, mult, shift, w, n_lat_tiles, tiles_per_seq, ctx_row):
    R = H.shape[0]
    rowmap = _mod_row_map(n_lat_tiles, tiles_per_seq, ctx_row)
    return pl.pallas_call(
        _na_proj_kernel,
        grid=(R // TM,),
        in_specs=[
            pl.BlockSpec((TM, D), lambda i: (i, 0)),
            pl.BlockSpec((1, 1, D), rowmap),
            pl.BlockSpec((1, 1, D), rowmap),
            _resident((D, 3 * D)),
        ],
        out_specs=pl.BlockSpec((TM, 3 * D), lambda i: (i, 0)),
        out_shape=jax.ShapeDtypeStruct((R, 3 * D), BF16),
        compiler_params=_params(),
        name="na_qkv_proj",
    )(H, mult, shift, w)


NA_ROWS_PER_STEP = 4


def _stack_heads(q):
    lane = lax.broadcasted_iota(jnp.int32, q.shape, 1)
    zero = jnp.zeros_like(q)
    return jnp.concatenate([jnp.where(lane < NA_DH, q, zero), jnp.where(lane >= NA_DH, q, zero)], axis=0)


def _softmax_pv(s_list, v_list):
    m = functools.reduce(jnp.maximum, [jnp.max(s, axis=1, keepdims=True) for s in s_list])
    p_list = [jnp.exp(s - m) for s in s_list]
    l = functools.reduce(jnp.add, [jnp.sum(p, axis=1, keepdims=True) for p in p_list])
    o = functools.reduce(jnp.add, [_dot(p.astype(BF16), v) for p, v in zip(p_list, v_list)])
    return o / l


def _unstack_heads(o):
    n = o.shape[0] // 2
    lane = lax.broadcasted_iota(jnp.int32, (n, 128), 1)
    return jnp.where(lane < NA_DH, o[:n], o[n:])


def _na_kernel(q_ref, k_ref, v_ref, kc_ref, vc_ref, qc_ref, tb_ref, o_ref, oc_ref, sw_ref, sc_ref, *, rows):
    kc = kc_ref[...]
    vc = vc_ref[...]
    win = NA_WIN_H * GRID_W
    n2 = 2 * GRID_W
    rps = NA_ROWS_PER_STEP
    n_groups = rows // rps

    def group_rows(g):
        rs = [g * rps + u for u in range(rps)]
        kr0 = [jnp.clip(r - NA_WIN_H // 2, 0, rows - NA_WIN_H) for r in rs]
        qrows = [pl.ds(pl.multiple_of(r * GRID_W, GRID_W), GRID_W) for r in rs]
        krows = [pl.ds(pl.multiple_of(k0 * GRID_W, GRID_W), win) for k0 in kr0]
        return rs, kr0, qrows, krows

    def scores(g, slot):
        rs, kr0, qrows, krows = group_rows(g)
        q2 = [_stack_heads(q_ref[qr, :]) for qr in qrows]
        s_w = [_dot_nt(q, k_ref[kr, :]) for q, kr in zip(q2, krows)]
        sc_ref[slot] = _dot_nt(jnp.concatenate(q2, axis=0), kc)
        for u, (s, r, k0) in enumerate(zip(s_w, rs, kr0)):
            bias = [tb_ref[0, j + NA_WIN_H - 1 - (r - k0)] for j in range(0, NA_WIN_H, 2)]
            sw_ref[slot, u] = s + jnp.concatenate(bias, axis=1)

    def softmax_pv(g, slot):
        _, _, qrows, krows = group_rows(g)
        s_w = [sw_ref[slot, u] for u in range(rps)]
        s_c = [sc_ref[slot, u * n2:(u + 1) * n2, :] for u in range(rps)]
        m = [jnp.maximum(jnp.max(a, axis=1, keepdims=True), jnp.max(b, axis=1, keepdims=True))
             for a, b in zip(s_w, s_c)]
        p_w = [jnp.exp(s - mm) for s, mm in zip(s_w, m)]
        p_c = [jnp.exp(s - mm) for s, mm in zip(s_c, m)]
        l = [jnp.sum(a, axis=1, keepdims=True) + jnp.sum(b, axis=1, keepdims=True) for a, b in zip(p_w, p_c)]
        o_c_all = _dot(jnp.concatenate([b.astype(BF16) for b in p_c], axis=0), vc)
        o = [_dot(a.astype(BF16), v_ref[kr, :]) + o_c_all[u * n2:(u + 1) * n2]
             for u, (a, kr) in enumerate(zip(p_w, krows))]
        for oo, ll, qr in zip(o, l, qrows):
            o_ref[qr, :] = _unstack_heads(oo / ll).astype(o_ref.dtype)

    scores(0, 0)

    def pair_body(i, carry):
        g = 2 * i
        scores(g + 1, 1)
        softmax_pv(g, 0)
        scores(jnp.minimum(g + 2, n_groups - 1), 0)
        softmax_pv(g + 1, 1)
        return carry

    lax.fori_loop(0, n_groups // 2, pair_body, 0)

    oc = _softmax_pv([_dot_nt(_stack_heads(qc_ref[...]), kc)], [vc])
    oc_ref[...] = _unstack_heads(oc).astype(oc_ref.dtype)


def _na_attention(Zq, tb, B, T, LC):
    rows = T // GRID_W
    assert rows % (2 * NA_ROWS_PER_STEP) == 0 and rows >= NA_WIN_H
    n_pairs = NA_H // 2
    score_slots = [pltpu.VMEM((2, NA_ROWS_PER_STEP, 2 * GRID_W, NA_WIN_H * GRID_W), F32),
                   pltpu.VMEM((2, NA_ROWS_PER_STEP * 2 * GRID_W, LC), F32)]
    ctx_blk0 = B * T // LC
    lat = lambda part: pl.BlockSpec((T, 128), lambda b, p: (b, part * n_pairs + p))
    ctx = lambda part: pl.BlockSpec((LC, 128), lambda b, p: (ctx_blk0 + b, part * n_pairs + p))
    return pl.pallas_call(
        functools.partial(_na_kernel, rows=rows),
        grid=(B, n_pairs),
        in_specs=[lat(0), lat(1), lat(2), ctx(1), ctx(2), ctx(0),
                  pl.BlockSpec((1, 2 * NA_WIN_H - 2, 2 * GRID_W, 2 * GRID_W), lambda b, p: (p, 0, 0, 0))],
        out_specs=[pl.BlockSpec((T, 128), lambda b, p: (b, p)),
                   pl.BlockSpec((LC, 128), lambda b, p: (b, p))],
        out_shape=[jax.ShapeDtypeStruct((B * T, D), BF16), jax.ShapeDtypeStruct((B * LC, D), BF16)],
        scratch_shapes=score_slots,
        compiler_params=_params(2),
        name="na_attention",
    )(Zq, Zq, Zq, Zq, Zq, Zq, tb)


def _na_bias_tables(rpb):
    c, kc = np.meshgrid(np.arange(GRID_W), np.arange(GRID_W), indexing="ij")
    start = np.clip(c - NA_WIN_W // 2, 0, GRID_W - NA_WIN_W)
    inside = (kc >= start) & (kc < start + NA_WIN_W)
    onehot = np.zeros((2 * NA_WIN_W - 1, GRID_W, GRID_W), np.float32)
    onehot[(kc - c + NA_WIN_W - 1)[inside], c[inside], kc[inside]] = 1.0
    toep = jnp.einsum("hdi,ick->hdck", rpb.astype(F32), onehot, precision=lax.Precision.HIGHEST)
    toep = jnp.where(inside[None, None], toep, -1e30)
    two = jnp.concatenate([toep[:, :-1], toep[:, 1:]], axis=-1)
    n_d = 2 * NA_WIN_H - 2
    return (two.reshape(NA_H // 2, 2, n_d, GRID_W, 2 * GRID_W).transpose(0, 2, 1, 3, 4)
            .reshape(NA_H // 2, n_d, 2 * GRID_W, 2 * GRID_W))


def _na_out_kernel(yl_ref, yc_ref, h_ref, gate_ref, w_ref, o_ref, *, n_lat_tiles):
    y = jnp.where(pl.program_id(0) < n_lat_tiles, yl_ref[...], yc_ref[...])
    o_ref[...] = h_ref[...] + gate_ref[0] * _dot(y, w_ref[...])


def _na_out(Y_lat, Y_ctx, H, gate, w, n_lat_tiles, tiles_per_seq, ctx_row):
    R = H.shape[0]
    rowmap = _mod_row_map(n_lat_tiles, tiles_per_seq, ctx_row)
    return pl.pallas_call(
        functools.partial(_na_out_kernel, n_lat_tiles=n_lat_tiles),
        grid=(R // TM,),
        in_specs=[
            pl.BlockSpec((TM, D), lambda i: (jnp.minimum(i, n_lat_tiles - 1), 0)),
            pl.BlockSpec((TM, D), lambda i: (jnp.maximum(i - n_lat_tiles, 0), 0)),
            pl.BlockSpec((TM, D), lambda i: (i, 0)),
            pl.BlockSpec((1, 1, D), rowmap),
            _resident((D, D)),
        ],
        out_specs=pl.BlockSpec((TM, D), lambda i: (i, 0)),
        out_shape=jax.ShapeDtypeStruct((R, D), F32),
        input_output_aliases={2: 0},
        compiler_params=_params(),
        name="na_out_proj",
    )(Y_lat, Y_ctx, H, gate, w)


def _ffn_kernel(hp_ref, h_ref, hn_ref, mult_ref, shift_ref, gate_ref, wup_ref, cw_ref, cb_ref, wd_ref, fnw_ref,
                o_ref, u_ref, act_ref, *, final, n_lat_rows, T, LC):
    x = h_ref[...]
    mult = mult_ref[0]
    shift = shift_ref[0]
    u_ref[0:HALO, :] = _norm_mod(hp_ref[...], mult, shift).astype(BF16)
    u_ref[HALO:HALO + TM, :] = _norm_mod(x, mult, shift).astype(BF16)
    u_ref[HALO + TM:, :] = _norm_mod(hn_ref[...], mult, shift).astype(BF16)

    r = pl.program_id(0) * TM + lax.broadcasted_iota(jnp.int32, (TM, 1), 0)
    is_lat = r < n_lat_rows
    pos = jnp.where(is_lat, r & (T - 1), r & (LC - 1))
    first = pos == 0
    last = pos == jnp.where(is_lat, T - 1, LC - 1)

    n_ext = TM + 2 * HALO
    for c in range(D_FF // FC):
        cols = slice(c * FC, (c + 1) * FC)
        a = _dot(u_ref[...], wup_ref[:, cols])
        a_prev = pltpu.roll(a, 1, 0)[HALO:HALO + TM]
        a_next = pltpu.roll(a, n_ext - 1, 0)[HALO:HALO + TM]
        a_mid = a[HALO:HALO + TM]
        conv = (cw_ref[0:1, cols] * jnp.where(first, 0.0, a_prev) + cw_ref[1:2, cols] * a_mid
                + cw_ref[2:3, cols] * jnp.where(last, 0.0, a_next) + cb_ref[:, cols])
        gt = _dot(u_ref[HALO:HALO + TM, :], wup_ref[:, D_FF + c * FC:D_FF + (c + 1) * FC])
        act_ref[:, cols] = (jax.nn.gelu(conv, approximate=True) * gt).astype(BF16)

    y = x + gate_ref[0] * _dot(act_ref[...], wd_ref[...])
    if final:
        y = _norm_mod(y, fnw_ref[...], 0.0)
    o_ref[...] = y


def _ffn(H, mult, shift, gate, wup, cw, cb, wd, fnw, *, final, n_rows_out, n_lat_tiles, tiles_per_seq,
         ctx_row, n_lat_rows, T, LC):
    R = H.shape[0]
    rowmap = _mod_row_map(n_lat_tiles, tiles_per_seq, ctx_row)
    per = TM // HALO
    last_halo_blk = R // HALO - 1
    kern = functools.partial(_ffn_kernel, final=final, n_lat_rows=n_lat_rows, T=T, LC=LC)
    return pl.pallas_call(
        kern,
        grid=(n_rows_out // TM,),
        in_specs=[
            pl.BlockSpec((HALO, D), lambda i: (jnp.maximum(i * per - 1, 0), 0)),
            pl.BlockSpec((TM, D), lambda i: (i, 0)),
            pl.BlockSpec((HALO, D), lambda i: (jnp.minimum((i + 1) * per, last_halo_blk), 0)),
            pl.BlockSpec((1, 1, D), rowmap),
            pl.BlockSpec((1, 1, D), rowmap),
            pl.BlockSpec((1, 1, D), rowmap),
            _resident((D, 2 * D_FF)),
            pl.BlockSpec((3, D_FF), lambda i: (0, 0)),
            pl.BlockSpec((1, D_FF), lambda i: (0, 0)),
            _resident((D_FF, D)),
            pl.BlockSpec((1, D), lambda i: (0, 0)),
        ],
        out_specs=pl.BlockSpec((TM, D), lambda i: (i, 0)),
        out_shape=jax.ShapeDtypeStruct((n_rows_out, D), F32),
        scratch_shapes=[pltpu.VMEM((TM + 2 * HALO, D), BF16), pltpu.VMEM((TM, D_FF), BF16)],
        compiler_params=_params(),
        name="conv_ffn_final" if final else "conv_ffn",
    )(H, H, H, mult, shift, gate, wup, cw, cb, wd, fnw)


def _rope_tables(T, head_dim):
    pos = jnp.arange(T, dtype=jnp.int32)
    row = (pos // GRID_W).astype(F32)
    col = (pos % GRID_W).astype(F32)
    n_freq = head_dim // 4
    inv_freq = ROPE_BASE ** (-jnp.arange(n_freq, dtype=F32) / n_freq)
    ang = jnp.concatenate([row[:, None] * inv_freq, col[:, None] * inv_freq], axis=-1)
    cos, sin = jnp.cos(ang), jnp.sin(ang)
    reps = 128 // head_dim
    c = jnp.tile(jnp.concatenate([cos, cos], axis=-1), (1, reps))
    s = jnp.tile(jnp.concatenate([-sin, sin], axis=-1), (1, reps))
    c = jnp.concatenate([c, jnp.ones((TM, 128), F32)], axis=0)
    s = jnp.concatenate([s, jnp.zeros((TM, 128), F32)], axis=0)
    return c, s


def kernel(x, c, ctx, c_ctx, mod_w, mod_b, norm_mix_w, norm_ffn_w, ev_w_in, ev_w_out, gla_a_up, gla_a_bias,
           gla_norm_w, ml_gate_bias, ml_norm_w, na_w_qkv, na_w_out, na_rpb, ffn_w_up, ffn_conv_w, ffn_conv_b,
           ffn_w_down, final_norm_w):
    B, T, _ = x.shape
    LC = ctx.shape[1]
    assert x.shape[2] == D and T % TM == 0 and (B * LC) % TM == 0 and B < N_MOD_ROWS
    assert T & (T - 1) == 0 and LC & (LC - 1) == 0 and T % GRID_W == 0
    n_lat_rows = B * T
    n_lat_tiles = n_lat_rows // TM
    tiles_per_seq = T // TM
    tile_kw = dict(n_lat_tiles=n_lat_tiles, tiles_per_seq=tiles_per_seq, ctx_row=B)

    H = jnp.concatenate([x.reshape(B * T, D), ctx.reshape(B * LC, D)], axis=0)
    R = H.shape[0]

    cvec = jnp.zeros((N_MOD_ROWS, D), F32).at[:B].set(c).at[B].set(c_ctx)
    mods = _modulation(cvec, mod_w, mod_b)
    ropes = _rope_tables(T, GLA_DK) + _rope_tables(T, ML_DH)

    out = None
    for layer in range(DEPTH):
        j = layer // 2
        last = layer == DEPTH - 1
        sh1, sc1, g1, sh2, sc2, g2 = [m.reshape(N_MOD_ROWS, 1, D) for m in jnp.split(mods[layer], 6, axis=-1)]
        mult1 = norm_mix_w[layer] * (1.0 + sc1)
        mult2 = norm_ffn_w[layer] * (1.0 + sc2)
        if layer % 2 == 0:
            w = ev_w_in[j]
            w_re = jnp.concatenate(
                [w[:, 0:1536], w[:, 1568:3616], w[:, 1536:1568], w[:, 3616:3632],
                 jnp.zeros((D, EVEN_N - 3632), F32)], axis=1).astype(BF16)
            Z = _even_proj(H, mult1, sh1, w_re, ropes, **tile_kw)
            aup = jnp.zeros((2, 128, GLA_H * GLA_DK), F32)
            aup = aup.at[0, 0:GLA_RANK].set(gla_a_up[j, 0]).at[1, GLA_RANK:2 * GLA_RANK].set(gla_a_up[j, 1])
            mgb = jnp.zeros((1, 128), F32).at[0, 2 * GLA_RANK:2 * GLA_RANK + 4 * ML_H].set(ml_gate_bias[j].reshape(-1))
            of, ob = _scan(Z, aup.astype(BF16), gla_a_bias[j].reshape(2, 1, -1), mgb, B, T, LC)
            H = _even_out(of, ob, Z, H, g1, gla_norm_w[j].reshape(1, -1), ml_norm_w[j].reshape(1, -1),
                          ev_w_out[j].astype(BF16), **tile_kw)
        else:
            Zq = _na_proj(H, mult1, sh1, na_w_qkv[j].astype(BF16), **tile_kw)
            o_lat, o_ctx = _na_attention(Zq, _na_bias_tables(na_rpb[j]), B, T, LC)
            H = _na_out(o_lat, o_ctx, H, g1, na_w_out[j].astype(BF16), **tile_kw)
        res = _ffn(H, mult2, sh2, g2, ffn_w_up[layer].astype(BF16), ffn_conv_w[layer],
                   ffn_conv_b[layer].reshape(1, -1), ffn_w_down[layer].astype(BF16), final_norm_w.reshape(1, -1),
                   final=last, n_rows_out=n_lat_rows if last else R, n_lat_rows=n_lat_rows, T=T, LC=LC, **tile_kw)
        if last:
            out = res
        else:
            H = res
    return out.reshape(B, T, D)
```

```python
import functools

import numpy as np
import jax
import jax.numpy as jnp
from jax import lax
from jax.experimental import pallas as pl
from jax.experimental.pallas import tpu as pltpu

F32 = jnp.float32
BF16 = jnp.bfloat16

D = 1024
GRID_W = 64
EPS = 1e-6
ROPE_BASE = 10000.0
CHUNK = 64
GLA_H, GLA_DK, GLA_DV, GLA_RANK, GLA_TAU = 4, 64, 128, 16, 16.0
ML_H, ML_DH = 4, 128
NA_H, NA_DH, NA_WIN_H, NA_WIN_W = 16, 64, 8, 16
D_FF = 2816
DEPTH = 4
N_MOD_ROWS = 16

TM = 1024
SUB = 512
SB = 256
HALO = 16
FC = 256
EVEN_N = 3712
SMALL_COL = 3584
VMEM_LIMIT = 58 * 1024 * 1024


def _dot(a, b):
    return jnp.dot(a, b, preferred_element_type=F32)


def _dot_nt(a, b):
    return lax.dot_general(a, b, (((1,), (1,)), ((), ())), preferred_element_type=F32)


def _dot_tn(a, b):
    return lax.dot_general(a, b, (((0,), (0,)), ((), ())), preferred_element_type=F32)


def _split_bf16(x):
    hi = x.astype(BF16)
    lo = (x - hi.astype(F32)).astype(BF16)
    return hi, lo


def _log_sigmoid(x):
    return jnp.minimum(x, 0.0) - jnp.log(1.0 + jnp.exp(-jnp.abs(x)))


def _sigmoid(x):
    return 1.0 / (1.0 + jnp.exp(-x))


def _params(n_axes=1):
    return pltpu.CompilerParams(dimension_semantics=("arbitrary",) * n_axes, vmem_limit_bytes=VMEM_LIMIT)


def _resident(shape):
    nd = len(shape)
    return pl.BlockSpec(shape, lambda *_: (0,) * nd, pipeline_mode=pl.Buffered(1))


def _mod_kernel(cv_ref, w_ref, b_ref, o_ref):
    a = cv_ref[...]
    a = a * _sigmoid(a)
    w = w_ref[0]
    a_hi, a_lo = _split_bf16(a)
    w_hi, w_lo = _split_bf16(w)
    o_ref[0] = _dot(a_hi, w_hi) + _dot(a_hi, w_lo) + _dot(a_lo, w_hi) + b_ref[0]


def _modulation(cvec, mod_w, mod_b):
    tn = 1536
    return pl.pallas_call(
        _mod_kernel,
        grid=(DEPTH, 6 * D // tn),
        in_specs=[
            pl.BlockSpec((N_MOD_ROWS, D), lambda l, j: (0, 0)),
            pl.BlockSpec((1, D, tn), lambda l, j: (l, 0, j)),
            pl.BlockSpec((1, 1, tn), lambda l, j: (l, 0, j)),
        ],
        out_specs=pl.BlockSpec((1, N_MOD_ROWS, tn), lambda l, j: (l, 0, j)),
        out_shape=jax.ShapeDtypeStruct((DEPTH, N_MOD_ROWS, 6 * D), F32),
        compiler_params=_params(2),
        name="adaln_table",
    )(cvec, mod_w, mod_b.reshape(DEPTH, 1, 6 * D))


def _norm_mod(x, mult, shift):
    ms = jnp.mean(x * x, axis=-1, keepdims=True)
    return x * lax.rsqrt(ms + EPS) * mult + shift


def _stream_operand(H, n_lat_tiles, tm=TM):
    if isinstance(H, tuple):
        specs = [pl.BlockSpec((tm, D), lambda i: (jnp.minimum(i, n_lat_tiles - 1), 0)),
                 pl.BlockSpec((tm, D), lambda i: (jnp.maximum(i - n_lat_tiles, 0), 0))]
        return specs, list(H), sum(h.shape[0] for h in H)
    return [pl.BlockSpec((tm, D), lambda i: (i, 0))], [H], H.shape[0]


def _stream_rows(h_refs, rows, n_lat_tiles):
    if len(h_refs) == 2:
        return jnp.where(pl.program_id(0) < n_lat_tiles, h_refs[0][rows, :], h_refs[1][rows, :])
    return h_refs[0][rows, :]


def _mod_row_map(n_lat_tiles, tiles_per_seq, ctx_row):
    def index_map(i):
        return (jnp.where(i < n_lat_tiles, i // tiles_per_seq, ctx_row), 0, 0)
    return index_map


def _rope_gla(x, c, s):
    lane = lax.broadcasted_iota(jnp.int32, x.shape, 1)
    swapped = jnp.where((lane % GLA_DK) < GLA_DK // 2, pltpu.roll(x, 128 - GLA_DK // 2, 1),
                        pltpu.roll(x, GLA_DK // 2, 1))
    return x * c + swapped * s


def _rope_ml(x, c, s):
    return x * c + pltpu.roll(x, ML_DH // 2, 1) * s


def _even_proj_kernel(*refs, n_lat_tiles, tm):
    h_refs = refs[:-8]
    mult_ref, shift_ref, w_ref, cg_ref, sg_ref, cm_ref, sm_ref, z_ref = refs[-8:]
    q_scale = {0: GLA_DK ** -0.5, 1: 1.0, 6: 1.0, 7: 1.0, 8: ML_DH ** -0.5, 9: ML_DH ** -0.5}
    for r0 in range(0, tm, SUB):
        rows = slice(r0, r0 + SUB)
        u = _norm_mod(_stream_rows(h_refs, rows, n_lat_tiles), mult_ref[0], shift_ref[0]).astype(BF16)
        for j in range(SMALL_COL // 256):
            z = _dot(u, w_ref[:, j * 256:(j + 1) * 256])
            if j in q_scale:
                rope = _rope_gla if j < 2 else _rope_ml
                c_ref, s_ref = (cg_ref, sg_ref) if j < 2 else (cm_ref, sm_ref)
                for half in range(2):
                    zz = rope(z[:, half * 128:(half + 1) * 128], c_ref[rows, :], s_ref[rows, :]) * q_scale[j]
                    z_ref[rows, j * 256 + half * 128:j * 256 + (half + 1) * 128] = zz
            else:
                z_ref[rows, j * 256:(j + 1) * 256] = z
        z_ref[rows, SMALL_COL:EVEN_N] = _dot(u, w_ref[:, SMALL_COL:EVEN_N])


def _even_proj(H, mult, shift, w, ropes, n_lat_rows, T, ctx_row):
    tm = SUB if isinstance(H, tuple) else TM
    n_lat_tiles, tiles_per_seq = n_lat_rows // tm, T // tm
    h_specs, h_args, R = _stream_operand(H, n_lat_tiles, tm)
    rowmap = _mod_row_map(n_lat_tiles, tiles_per_seq, ctx_row)
    ropemap = lambda i: (jnp.where(i < n_lat_tiles, i % tiles_per_seq, tiles_per_seq), 0)
    return pl.pallas_call(
        functools.partial(_even_proj_kernel, n_lat_tiles=n_lat_tiles, tm=tm),
        grid=(R // tm,),
        in_specs=h_specs + [
            pl.BlockSpec((1, 1, D), rowmap),
            pl.BlockSpec((1, 1, D), rowmap),
            _resident((D, EVEN_N)),
        ] + [pl.BlockSpec((tm, 128), ropemap)] * 4,
        out_specs=pl.BlockSpec((tm, EVEN_N), lambda i: (i, 0)),
        out_shape=jax.ShapeDtypeStruct((R, EVEN_N), F32),
        compiler_params=_params(),
        name="even_in_proj",
    )(*h_args, mult, shift, w, *ropes)


def _chunk_cummax(x, z, tmod):
    n = x.shape[0]
    k = 1
    while k < CHUNK:
        if z == 0:
            shifted, valid = pltpu.roll(x, k, 0), tmod >= k
        else:
            shifted, valid = pltpu.roll(x, n - k, 0), tmod < CHUNK - k
        x = jnp.maximum(x, jnp.where(valid, shifted, -jnp.inf))
        k *= 2
    return x


def _scan_kernel(*args):
    in_refs = (args[0:7], args[7:14])
    aup_ref, abias_ref, mgb_ref, of_ref, ob_ref, s_ref, c_ref, m_ref = args[14:]
    out_refs = (of_ref, ob_ref)
    dirs = (0, 1)
    n_chunks = SB // CHUNK
    order = (tuple(range(n_chunks)), tuple(reversed(range(n_chunks))))
    edge = lambda z, c: c * CHUNK + (CHUNK - 1 if z == 0 else 0)
    rows_of = lambda c: slice(c * CHUNK, (c + 1) * CHUNK)

    @pl.when(pl.program_id(1) == 0)
    def _():
        s_ref[...] = jnp.zeros_like(s_ref)
        c_ref[...] = jnp.zeros_like(c_ref)
        m_ref[...] = jnp.zeros_like(m_ref)

    ti = lax.broadcasted_iota(jnp.int32, (SB, SB), 0)
    si = lax.broadcasted_iota(jnp.int32, (SB, SB), 1)
    same = (ti // CHUNK) == (si // CHUNK)
    mask = [same & (si <= ti), same & (si >= ti)]
    tri = [x.astype(BF16) for x in mask]

    def chunk_totals(run, z):
        return jnp.concatenate([jnp.broadcast_to(run[edge(z, c):edge(z, c) + 1, :], (CHUNK, run.shape[1]))
                                for c in range(n_chunks)], axis=0)

    tmod = lax.broadcasted_iota(jnp.int32, (SB, 128), 0) % CHUNK
    lane256 = lax.broadcasted_iota(jnp.int32, (SB, GLA_H * GLA_DK), 1)
    in_head = [(lane256 // GLA_DK) == h for h in range(GLA_H)]
    lane128 = lax.broadcasted_iota(jnp.int32, (1, 128), 1)
    gate_col = lambda z, h: 2 * GLA_RANK + z * 2 * ML_H + h
    ones_b = jnp.ones((SB, ML_DH), BF16)

    P = []
    for z in dirs:
        gq_ref, gk_ref, gv_ref, sm_ref, mq_ref, mk_ref, mv_ref = in_refs[z]
        small = sm_ref[...]
        a_pre = _dot(small.astype(BF16), aup_ref[z]) + abias_ref[z]
        la_hi, la_lo = _split_bf16(_log_sigmoid(a_pre) * (1.0 / GLA_TAU))
        b = _dot(tri[z], la_hi) + _dot(tri[z], la_lo)
        gb = chunk_totals(b, z)
        q = gq_ref[...]
        k = gk_ref[...]
        gates = small + mgb_ref[...]
        lf_hi, lf_lo = _split_bf16(_log_sigmoid(gates))
        bc = pltpu.roll(_dot(tri[z], lf_hi) + _dot(tri[z], lf_lo), 128 - ML_H, 1)
        gc = chunk_totals(bc, z)
        w = gates - bc
        cm = _chunk_cummax(w, z, tmod)
        P.append(dict(
            q_dec=(q * jnp.exp(b)).astype(BF16), k_dec=(k * jnp.exp(-b)).astype(BF16),
            k_end=(k * jnp.exp(gb - b)).astype(BF16), gb=gb, bc=bc, gc=gc, w=w, cm=cm, w_t=w.T))

    G, M = {}, {}
    for z in dirs:
        gq_ref, gk_ref, gv_ref, sm_ref, mq_ref, mk_ref, mv_ref = in_refs[z]
        p = P[z]
        for h in range(GLA_H):
            qm = jnp.where(in_head[h], p["q_dec"], jnp.zeros_like(p["q_dec"]))
            km = jnp.where(in_head[h], p["k_end"], jnp.zeros_like(p["k_end"]))
            v = gv_ref[:, h * GLA_DV:(h + 1) * GLA_DV].astype(BF16)
            G[z, h] = dict(qm=qm, km=km, v=v, sc=_dot_nt(qm, p["k_dec"]))
        for h in range(ML_H):
            fi = gate_col(z, h)
            cols = slice(h * ML_DH, (h + 1) * ML_DH)
            d = jnp.exp(jnp.where(mask[z], p["w_t"][fi:fi + 1, :] - p["cm"][:, fi:fi + 1], -jnp.inf))
            qb = mq_ref[:, cols].astype(BF16)
            va = jnp.concatenate([mv_ref[:, cols].astype(BF16), ones_b], axis=1)
            M[z, h] = dict(qb=qb, va=va, d=d, qk=_dot_nt(qb, mk_ref[:, cols].astype(BF16)))
    for z in dirs:
        for h in range(GLA_H):
            e = G[z, h]
            e["intra"] = _dot(jnp.where(mask[z], e["sc"], 0.0).astype(BF16), e["v"])
        for h in range(ML_H):
            e = M[z, h]
            e["intra"] = _dot((e["qk"] * e["d"]).astype(BF16), e["va"])
    for z in dirs:
        mk_ref = in_refs[z][5]
        p = P[z]
        for c in range(n_chunks):
            r = rows_of(c)
            wmax = p["cm"][edge(z, c):edge(z, c) + 1, :]
            w_end = jnp.exp(p["w"][r] - wmax)
            p["m_loc", c] = wmax + p["gc"][c * CHUNK:c * CHUNK + 1, :]
            for h in range(GLA_H):
                e = G[z, h]
                e["kv", c] = _dot_tn(e["v"][r], e["km"][r])
            for h in range(ML_H):
                e = M[z, h]
                fi = gate_col(z, h)
                kw = (mk_ref[r, h * ML_DH:(h + 1) * ML_DH] * w_end[:, fi:fi + 1]).astype(BF16)
                e["cl", c] = _dot_tn(kw, e["va"][r])

    S = {(z, h): s_ref[z, h] for z in dirs for h in range(GLA_H)}
    C = {(z, h): c_ref[z, h] for z in dirs for h in range(ML_H)}
    m_old = [m_ref[z] for z in dirs]
    head_lanes = [functools.reduce(jnp.logical_or, [lane128 == gate_col(z, h) for h in range(ML_H)]) for z in dirs]
    for step in range(n_chunks):
        g_inter, m_inter = {}, {}
        for z in dirs:
            r = rows_of(order[z][step])
            for h in range(GLA_H):
                g_inter[z, h] = _dot_nt(G[z, h]["qm"][r], S[z, h].astype(BF16))
            for h in range(ML_H):
                m_inter[z, h] = _dot(M[z, h]["qb"][r], C[z, h].astype(BF16))
        for z in dirs:
            c = order[z][step]
            r = rows_of(c)
            p = P[z]
            decay = jnp.exp(p["gb"][c * CHUNK:c * CHUNK + 1, :])
            for h in range(GLA_H):
                o = G[z, h]["intra"][r] + g_inter[z, h]
                out_refs[z][r, h * GLA_DV:(h + 1) * GLA_DV] = o.astype(out_refs[z].dtype)
                S[z, h] = decay * S[z, h] + G[z, h]["kv", c]
            mx = jnp.maximum(m_old[z], p["cm"][r])
            alpha = jnp.exp(p["cm"][r] - mx)
            inter_w = jnp.exp(m_old[z] - mx)
            floor = jnp.exp(-(p["bc"][r] + mx))
            g_c = p["gc"][c * CHUNK:c * CHUNK + 1, :]
            m_new = jnp.where(head_lanes[z], jnp.maximum(g_c + m_old[z], p["m_loc", c]), 0.0)
            a_sc = jnp.exp(g_c + m_old[z] - m_new)
            w_sc = jnp.exp(p["m_loc", c] - m_new)
            for h in range(ML_H):
                fi = gate_col(z, h)
                al, iw = alpha[:, fi:fi + 1], inter_w[:, fi:fi + 1]
                tot = al * M[z, h]["intra"][r] + iw * m_inter[z, h]
                den = jnp.maximum(jnp.abs(tot[:, ML_DH:]), floor[:, fi:fi + 1])
                col = GLA_H * GLA_DV + h * ML_DH
                out_refs[z][r, col:col + ML_DH] = (tot[:, :ML_DH] / den).astype(out_refs[z].dtype)
                C[z, h] = a_sc[:, fi:fi + 1] * C[z, h] + w_sc[:, fi:fi + 1] * M[z, h]["cl", c]
            m_old[z] = m_new
    for z in dirs:
        for h in range(GLA_H):
            s_ref[z, h] = S[z, h]
        for h in range(ML_H):
            c_ref[z, h] = C[z, h]
        m_ref[z] = m_old[z]


def _scan(Z, aup, abias, mgb, B, T, LC):
    R = Z.shape[0]
    assert LC == SB and T % SB == 0
    nb = T // SB
    ctx0 = B * nb
    fmap = lambda b, s: jnp.where(s == 0, ctx0 + b, b * nb + s - 1)
    bmap = lambda b, s: jnp.where(s == 0, ctx0 + b, b * nb + nb - s)

    def specs(rowmap):
        col = lambda width, blk: pl.BlockSpec((SB, width), lambda b, s: (rowmap(b, s), blk))
        return [col(256, 0), col(256, 1), col(512, 1), col(128, SMALL_COL // 128),
                col(512, 3), col(512, 4), col(512, 5)]

    out_spec = lambda rowmap: pl.BlockSpec((SB, D), lambda b, s: (rowmap(b, s), 0))
    return pl.pallas_call(
        _scan_kernel,
        grid=(B, nb + 1),
        in_specs=specs(fmap) + specs(bmap) + [
            pl.BlockSpec((2, 128, 256), lambda b, s: (0, 0, 0)),
            pl.BlockSpec((2, 1, 256), lambda b, s: (0, 0, 0)),
            pl.BlockSpec((1, 128), lambda b, s: (0, 0)),
        ],
        out_specs=[out_spec(fmap), out_spec(bmap)],
        out_shape=[jax.ShapeDtypeStruct((R, D), BF16)] * 2,
        scratch_shapes=[
            pltpu.VMEM((2, GLA_H, GLA_DV, GLA_H * GLA_DK), F32),
            pltpu.VMEM((2, ML_H, ML_DH, 2 * ML_DH), F32),
            pltpu.VMEM((2, 1, 128), F32),
        ],
        compiler_params=_params(2),
        name="gla_mlstm_scan",
    )(*([Z] * 14), aup, abias, mgb)


def _even_out_kernel(*refs, n_lat_tiles):
    of_ref, ob_ref, gg_ref, mo_ref = refs[:4]
    h_refs = refs[4:-6]
    gate_ref, gnw_ref, mnw_ref, w_ref, o_ref, y_ref = refs[-6:]
    for r0 in range(0, TM, SUB):
        rows = slice(r0, r0 + SUB)
        for hh in range(GLA_H + ML_H):
            cols = slice(hh * 128, (hh + 1) * 128)
            x = of_ref[rows, cols].astype(F32) + ob_ref[rows, cols].astype(F32)
            ms = jnp.mean(x * x, axis=-1, keepdims=True)
            if hh < GLA_H:
                gt = gg_ref[rows, cols]
                y = x * lax.rsqrt(ms + EPS) * gnw_ref[...] * (gt * _sigmoid(gt))
            else:
                gt = mo_ref[rows, (hh - GLA_H) * 128:(hh - GLA_H + 1) * 128]
                y = x * lax.rsqrt(ms + EPS) * mnw_ref[...] * _sigmoid(gt)
            y_ref[rows, cols] = y.astype(BF16)
        o_ref[rows, :] = _stream_rows(h_refs, rows, n_lat_tiles) + gate_ref[0] * _dot(y_ref[rows, :], w_ref[...])


def _even_out(of, ob, Z, H, gate, gnw, mnw, w, n_lat_tiles, tiles_per_seq, ctx_row):
    h_specs, h_args, R = _stream_operand(H, n_lat_tiles)
    rowmap = _mod_row_map(n_lat_tiles, tiles_per_seq, ctx_row)
    return pl.pallas_call(
        functools.partial(_even_out_kernel, n_lat_tiles=n_lat_tiles),
        grid=(R // TM,),
        in_specs=[
            pl.BlockSpec((TM, D), lambda i: (i, 0)),
            pl.BlockSpec((TM, D), lambda i: (i, 0)),
            pl.BlockSpec((TM, 512), lambda i: (i, 2)),
            pl.BlockSpec((TM, 512), lambda i: (i, 6)),
        ] + h_specs + [
            pl.BlockSpec((1, 1, D), rowmap),
            pl.BlockSpec((1, 128), lambda i: (0, 0)),
            pl.BlockSpec((1, 128), lambda i: (0, 0)),
            _resident((D, D)),
        ],
        out_specs=pl.BlockSpec((TM, D), lambda i: (i, 0)),
        out_shape=jax.ShapeDtypeStruct((R, D), F32),
        scratch_shapes=[pltpu.VMEM((TM, D), BF16)],
        input_output_aliases={4: 0} if len(h_args) == 1 else {},
        compiler_params=_params(),
        name="even_out_proj",
    )(of, ob, Z, Z, *h_args, gate, gnw, mnw, w)


def _na_proj_kernel(h_ref, mult_ref, shift_ref, w_ref, z_ref):
    for r0 in range(0, TM, SUB):
        rows = slice(r0, r0 + SUB)
        u = _norm_mod(h_ref[rows, :], mult_ref[0], shift_ref[0]).astype(BF16)
        for j in range(3 * D // 256):
            z = _dot(u, w_ref[:, j * 256:(j + 1) * 256])
            if j < D // 256:
                z = z * (NA_DH ** -0.5)
            z_ref[rows, j * 256:(j + 1) * 256] = z.astype(BF16)


def _na_proj(H, mult, shift, w, n_lat_tiles, tiles_per_seq, ctx_row):
    R = H.shape[0]
    rowmap = _mod_row_map(n_lat_tiles, tiles_per_seq, ctx_row)
    return pl.pallas_call(
        _na_proj_kernel,
        grid=(R // TM,),
        in_specs=[
            pl.BlockSpec((TM, D), lambda i: (i, 0)),
            pl.BlockSpec((1, 1, D), rowmap),
            pl.BlockSpec((1, 1, D), rowmap),
            _resident((D, 3 * D)),
        ],
        out_specs=pl.BlockSpec((TM, 3 * D), lambda i: (i, 0)),
        out_shape=jax.ShapeDtypeStruct((R, 3 * D), BF16),
        compiler_params=_params(),
        name="na_qkv_proj",
    )(H, mult, shift, w)


NA_ROWS_PER_STEP = 4


def _stack_heads(q):
    lane = lax.broadcasted_iota(jnp.int32, q.shape, 1)
    zero = jnp.zeros_like(q)
    return jnp.concatenate([jnp.where(lane < NA_DH, q, zero), jnp.where(lane >= NA_DH, q, zero)], axis=0)


def _softmax_pv(s_list, v_list):
    m = functools.reduce(jnp.maximum, [jnp.max(s, axis=1, keepdims=True) for s in s_list])
    p_list = [jnp.exp(s - m) for s in s_list]
    l = functools.reduce(jnp.add, [jnp.sum(p, axis=1, keepdims=True) for p in p_list])
    o = functools.reduce(jnp.add, [_dot(p.astype(BF16), v) for p, v in zip(p_list, v_list)])
    return o / l


def _unstack_heads(o):
    n = o.shape[0] // 2
    lane = lax.broadcasted_iota(jnp.int32, (n, 128), 1)
    return jnp.where(lane < NA_DH, o[:n], o[n:])


def _na_kernel(q_ref, k_ref, v_ref, kc_ref, vc_ref, qc_ref, tb_ref, o_ref, oc_ref, sw_ref, sc_ref, *, rows):
    kc = kc_ref[...]
    vc = vc_ref[...]
    win = NA_WIN_H * GRID_W
    n2 = 2 * GRID_W
    rps = NA_ROWS_PER_STEP
    n_groups = rows // rps

    def group_rows(g):
        rs = [g * rps + u for u in range(rps)]
        kr0 = [jnp.clip(r - NA_WIN_H // 2, 0, rows - NA_WIN_H) for r in rs]
        qrows = [pl.ds(pl.multiple_of(r * GRID_W, GRID_W), GRID_W) for r in rs]
        krows = [pl.ds(pl.multiple_of(k0 * GRID_W, GRID_W), win) for k0 in kr0]
        return rs, kr0, qrows, krows

    def scores(g, slot):
        rs, kr0, qrows, krows = group_rows(g)
        q2 = [_stack_heads(q_ref[qr, :]) for qr in qrows]
        s_w = [_dot_nt(q, k_ref[kr, :]) for q, kr in zip(q2, krows)]
        sc_ref[slot] = _dot_nt(jnp.concatenate(q2, axis=0), kc)
        for u, (s, r, k0) in enumerate(zip(s_w, rs, kr0)):
            bias = [tb_ref[0, j + NA_WIN_H - 1 - (r - k0)] for j in range(0, NA_WIN_H, 2)]
            sw_ref[slot, u] = s + jnp.concatenate(bias, axis=1)

    def softmax_pv(g, slot):
        _, _, qrows, krows = group_rows(g)
        s_w = [sw_ref[slot, u] for u in range(rps)]
        s_c = [sc_ref[slot, u * n2:(u + 1) * n2, :] for u in range(rps)]
        m = [jnp.maximum(jnp.max(a, axis=1, keepdims=True), jnp.max(b, axis=1, keepdims=True))
             for a, b in zip(s_w, s_c)]
        p_w = [jnp.exp(s - mm) for s, mm in zip(s_w, m)]
        p_c = [jnp.exp(s - mm) for s, mm in zip(s_c, m)]
        l = [jnp.sum(a, axis=1, keepdims=True) + jnp.sum(b, axis=1, keepdims=True) for a, b in zip(p_w, p_c)]
        o_c_all = _dot(jnp.concatenate([b.astype(BF16) for b in p_c], axis=0), vc)
        o = [_dot(a.astype(BF16), v_ref[kr, :]) + o_c_all[u * n2:(u + 1) * n2]
             for u, (a, kr) in enumerate(zip(p_w, krows))]
        for oo, ll, qr in zip(o, l, qrows):
            o_ref[qr, :] = _unstack_heads(oo / ll).astype(o_ref.dtype)

    scores(0, 0)

    def pair_body(i, carry):
        g = 2 * i
        scores(g + 1, 1)
        softmax_pv(g, 0)
        scores(jnp.minimum(g + 2, n_groups - 1), 0)
        softmax_pv(g + 1, 1)
        return carry

    lax.fori_loop(0, n_groups // 2, pair_body, 0)

    oc = _softmax_pv([_dot_nt(_stack_heads(qc_ref[...]), kc)], [vc])
    oc_ref[...] = _unstack_heads(oc).astype(oc_ref.dtype)


def _na_attention(Zq, tb, B, T, LC):
    rows = T // GRID_W
    assert rows % (2 * NA_ROWS_PER_STEP) == 0 and rows >= NA_WIN_H
    n_pairs = NA_H // 2
    score_slots = [pltpu.VMEM((2, NA_ROWS_PER_STEP, 2 * GRID_W, NA_WIN_H * GRID_W), F32),
                   pltpu.VMEM((2, NA_ROWS_PER_STEP * 2 * GRID_W, LC), F32)]
    ctx_blk0 = B * T // LC
    lat = lambda part: pl.BlockSpec((T, 128), lambda b, p: (b, part * n_pairs + p))
    ctx = lambda part: pl.BlockSpec((LC, 128), lambda b, p: (ctx_blk0 + b, part * n_pairs + p))
    return pl.pallas_call(
        functools.partial(_na_kernel, rows=rows),
        grid=(B, n_pairs),
        in_specs=[lat(0), lat(1), lat(2), ctx(1), ctx(2), ctx(0),
                  pl.BlockSpec((1, 2 * NA_WIN_H - 2, 2 * GRID_W, 2 * GRID_W), lambda b, p: (p, 0, 0, 0))],
        out_specs=[pl.BlockSpec((T, 128), lambda b, p: (b, p)),
                   pl.BlockSpec((LC, 128), lambda b, p: (b, p))],
        out_shape=[jax.ShapeDtypeStruct((B * T, D), BF16), jax.ShapeDtypeStruct((B * LC, D), BF16)],
        scratch_shapes=score_slots,
        compiler_params=_params(2),
        name="na_attention",
    )(Zq, Zq, Zq, Zq, Zq, Zq, tb)


def _na_bias_tables(rpb):
    c, kc = np.meshgrid(np.arange(GRID_W), np.arange(GRID_W), indexing="ij")
    start = np.clip(c - NA_WIN_W // 2, 0, GRID_W - NA_WIN_W)
    inside = (kc >= start) & (kc < start + NA_WIN_W)
    onehot = np.zeros((2 * NA_WIN_W - 1, GRID_W, GRID_W), np.float32)
    onehot[(kc - c + NA_WIN_W - 1)[inside], c[inside], kc[inside]] = 1.0
    toep = jnp.einsum("hdi,ick->hdck", rpb.astype(F32), onehot, precision=lax.Precision.HIGHEST)
    toep = jnp.where(inside[None, None], toep, -1e30)
    two = jnp.concatenate([toep[:, :-1], toep[:, 1:]], axis=-1)
    n_d = 2 * NA_WIN_H - 2
    return (two.reshape(NA_H // 2, 2, n_d, GRID_W, 2 * GRID_W).transpose(0, 2, 1, 3, 4)
            .reshape(NA_H // 2, n_d, 2 * GRID_W, 2 * GRID_W))


def _na_out_kernel(yl_ref, yc_ref, h_ref, gate_ref, w_ref, o_ref, *, n_lat_tiles):
    y = jnp.where(pl.program_id(0) < n_lat_tiles, yl_ref[...], yc_ref[...])
    o_ref[...] = h_ref[...] + gate_ref[0] * _dot(y, w_ref[...])


def _na_out(Y_lat, Y_ctx, H, gate, w, n_lat_tiles, tiles_per_seq, ctx_row):
    R = H.shape[0]
    rowmap = _mod_row_map(n_lat_tiles, tiles_per_seq, ctx_row)
    return pl.pallas_call(
        functools.partial(_na_out_kernel, n_lat_tiles=n_lat_tiles),
        grid=(R // TM,),
        in_specs=[
            pl.BlockSpec((TM, D), lambda i: (jnp.minimum(i, n_lat_tiles - 1), 0)),
            pl.BlockSpec((TM, D), lambda i: (jnp.maximum(i - n_lat_tiles, 0), 0)),
            pl.BlockSpec((TM, D), lambda i: (i, 0)),
            pl.BlockSpec((1, 1, D), rowmap),
            _resident((D, D)),
        ],
        out_specs=pl.BlockSpec((TM, D), lambda i: (i, 0)),
        out_shape=jax.ShapeDtypeStruct((R, D), F32),
        input_output_aliases={2: 0},
        compiler_params=_params(),
        name="na_out_proj",
    )(Y_lat, Y_ctx, H, gate, w)


def _ffn_kernel(hp_ref, h_ref, hn_ref, mult_ref, shift_ref, gate_ref, wup_ref, cw_ref, cb_ref, wd_ref, fnw_ref,
                o_ref, u_ref, act_ref, *, final, n_lat_rows, T, LC):
    mult = mult_ref[0]
    shift = shift_ref[0]
    n_ext = TM + 2 * HALO
    cut = n_ext // 2
    half = TM // 2
    u_ref[0:HALO, :] = _norm_mod(hp_ref[...], mult, shift).astype(BF16)
    u_ref[HALO:cut, :] = _norm_mod(h_ref[0:half, :], mult, shift).astype(BF16)
    u_ref[cut:HALO + TM, :] = _norm_mod(h_ref[half:TM, :], mult, shift).astype(BF16)
    u_ref[HALO + TM:, :] = _norm_mod(hn_ref[...], mult, shift).astype(BF16)

    r = pl.program_id(0) * TM + lax.broadcasted_iota(jnp.int32, (TM, 1), 0)
    is_lat = r < n_lat_rows
    pos = jnp.where(is_lat, r & (T - 1), r & (LC - 1))
    first = pos == 0
    last = pos == jnp.where(is_lat, T - 1, LC - 1)

    for c in range(D_FF // FC):
        cols = slice(c * FC, (c + 1) * FC)
        w_a = wup_ref[:, cols]
        a = jnp.concatenate([_dot(u_ref[0:cut, :], w_a), _dot(u_ref[cut:n_ext, :], w_a)], axis=0)
        a_prev = pltpu.roll(a, 1, 0)[HALO:HALO + TM]
        a_next = pltpu.roll(a, n_ext - 1, 0)[HALO:HALO + TM]
        a_mid = a[HALO:HALO + TM]
        conv = (cw_ref[0:1, cols] * jnp.where(first, 0.0, a_prev) + cw_ref[1:2, cols] * a_mid
                + cw_ref[2:3, cols] * jnp.where(last, 0.0, a_next) + cb_ref[:, cols])
        w_g = wup_ref[:, D_FF + c * FC:D_FF + (c + 1) * FC]
        gt = jnp.concatenate([_dot(u_ref[HALO:cut, :], w_g), _dot(u_ref[cut:HALO + TM, :], w_g)], axis=0)
        act_ref[:, cols] = (jax.nn.gelu(conv, approximate=True) * gt).astype(BF16)

    for r0 in range(0, TM, SUB):
        rows = slice(r0, r0 + SUB)
        y = h_ref[rows, :] + gate_ref[0] * _dot(act_ref[rows, :], wd_ref[...])
        if final:
            y = _norm_mod(y, fnw_ref[...], 0.0)
        o_ref[rows, :] = y


def _ffn(H, mult, shift, gate, wup, cw, cb, wd, fnw, *, final, n_rows_out, n_lat_tiles, tiles_per_seq,
         ctx_row, n_lat_rows, T, LC):
    R = H.shape[0]
    rowmap = _mod_row_map(n_lat_tiles, tiles_per_seq, ctx_row)
    per = TM // HALO
    last_halo_blk = R // HALO - 1
    kern = functools.partial(_ffn_kernel, final=final, n_lat_rows=n_lat_rows, T=T, LC=LC)
    return pl.pallas_call(
        kern,
        grid=(n_rows_out // TM,),
        in_specs=[
            pl.BlockSpec((HALO, D), lambda i: (jnp.maximum(i * per - 1, 0), 0)),
            pl.BlockSpec((TM, D), lambda i: (i, 0)),
            pl.BlockSpec((HALO, D), lambda i: (jnp.minimum((i + 1) * per, last_halo_blk), 0)),
            pl.BlockSpec((1, 1, D), rowmap),
            pl.BlockSpec((1, 1, D), rowmap),
            pl.BlockSpec((1, 1, D), rowmap),
            _resident((D, 2 * D_FF)),
            pl.BlockSpec((3, D_FF), lambda i: (0, 0)),
            pl.BlockSpec((1, D_FF), lambda i: (0, 0)),
            _resident((D_FF, D)),
            pl.BlockSpec((1, D), lambda i: (0, 0)),
        ],
        out_specs=pl.BlockSpec((TM, D), lambda i: (i, 0)),
        out_shape=jax.ShapeDtypeStruct((n_rows_out, D), F32),
        scratch_shapes=[pltpu.VMEM((TM + 2 * HALO, D), BF16), pltpu.VMEM((TM, D_FF), BF16)],
        compiler_params=_params(),
        name="conv_ffn_final" if final else "conv_ffn",
    )(H, H, H, mult, shift, gate, wup, cw, cb, wd, fnw)


def _rope_tables(T, head_dim):
    pos = jnp.arange(T, dtype=jnp.int32)
    row = (pos // GRID_W).astype(F32)
    col = (pos % GRID_W).astype(F32)
    n_freq = head_dim // 4
    inv_freq = ROPE_BASE ** (-jnp.arange(n_freq, dtype=F32) / n_freq)
    ang = jnp.concatenate([row[:, None] * inv_freq, col[:, None] * inv_freq], axis=-1)
    cos, sin = jnp.cos(ang), jnp.sin(ang)
    reps = 128 // head_dim
    c = jnp.tile(jnp.concatenate([cos, cos], axis=-1), (1, reps))
    s = jnp.tile(jnp.concatenate([-sin, sin], axis=-1), (1, reps))
    c = jnp.concatenate([c, jnp.ones((TM, 128), F32)], axis=0)
    s = jnp.concatenate([s, jnp.zeros((TM, 128), F32)], axis=0)
    return c, s


def kernel(x, c, ctx, c_ctx, mod_w, mod_b, norm_mix_w, norm_ffn_w, ev_w_in, ev_w_out, gla_a_up, gla_a_bias,
           gla_norm_w, ml_gate_bias, ml_norm_w, na_w_qkv, na_w_out, na_rpb, ffn_w_up, ffn_conv_w, ffn_conv_b,
           ffn_w_down, final_norm_w):
    B, T, _ = x.shape
    LC = ctx.shape[1]
    assert x.shape[2] == D and T % TM == 0 and (B * LC) % TM == 0 and B < N_MOD_ROWS
    assert T & (T - 1) == 0 and LC & (LC - 1) == 0 and T % GRID_W == 0
    n_lat_rows = B * T
    n_lat_tiles = n_lat_rows // TM
    tiles_per_seq = T // TM
    tile_kw = dict(n_lat_tiles=n_lat_tiles, tiles_per_seq=tiles_per_seq, ctx_row=B)

    H = (x.reshape(B * T, D), ctx.reshape(B * LC, D))
    R = n_lat_rows + B * LC

    cvec = jnp.zeros((N_MOD_ROWS, D), F32).at[:B].set(c).at[B].set(c_ctx)
    mods = _modulation(cvec, mod_w, mod_b)
    ropes = _rope_tables(T, GLA_DK) + _rope_tables(T, ML_DH)

    out = None
    for layer in range(DEPTH):
        j = layer // 2
        last = layer == DEPTH - 1
        sh1, sc1, g1, sh2, sc2, g2 = [m.reshape(N_MOD_ROWS, 1, D) for m in jnp.split(mods[layer], 6, axis=-1)]
        mult1 = norm_mix_w[layer] * (1.0 + sc1)
        mult2 = norm_ffn_w[layer] * (1.0 + sc2)
        if layer % 2 == 0:
            w = ev_w_in[j]
            w_re = jnp.concatenate(
                [w[:, 0:1536], w[:, 1568:3616], w[:, 1536:1568], w[:, 3616:3632],
                 jnp.zeros((D, EVEN_N - 3632), F32)], axis=1).astype(BF16)
            Z = _even_proj(H, mult1, sh1, w_re, ropes, n_lat_rows, T, B)
            aup = jnp.zeros((2, 128, GLA_H * GLA_DK), F32)
            aup = aup.at[0, 0:GLA_RANK].set(gla_a_up[j, 0]).at[1, GLA_RANK:2 * GLA_RANK].set(gla_a_up[j, 1])
            mgb = jnp.zeros((1, 128), F32).at[0, 2 * GLA_RANK:2 * GLA_RANK + 4 * ML_H].set(ml_gate_bias[j].reshape(-1))
            of, ob = _scan(Z, aup.astype(BF16), gla_a_bias[j].reshape(2, 1, -1), mgb, B, T, LC)
            H = _even_out(of, ob, Z, H, g1, gla_norm_w[j].reshape(1, -1), ml_norm_w[j].reshape(1, -1),
                          ev_w_out[j].astype(BF16), **tile_kw)
        else:
            Zq = _na_proj(H, mult1, sh1, na_w_qkv[j].astype(BF16), **tile_kw)
            o_lat, o_ctx = _na_attention(Zq, _na_bias_tables(na_rpb[j]), B, T, LC)
            H = _na_out(o_lat, o_ctx, H, g1, na_w_out[j].astype(BF16), **tile_kw)
        res = _ffn(H, mult2, sh2, g2, ffn_w_up[layer].astype(BF16), ffn_conv_w[layer],
                   ffn_conv_b[layer].reshape(1, -1), ffn_w_down[layer].astype(BF16), final_norm_w.reshape(1, -1),
                   final=last, n_rows_out=n_lat_rows if last else R, n_lat_rows=n_lat_rows, T=T, LC=LC, **tile_kw)
        if last:
            out = res
        else:
            H = res
    return out.reshape(B, T, D)
```

```python
import functools

import numpy as np
import jax
import jax.numpy as jnp
from jax import lax
from jax.experimental import pallas as pl
from jax.experimental.pallas import tpu as pltpu

F32 = jnp.float32
BF16 = jnp.bfloat16

D = 1024
GRID_W = 64
EPS = 1e-6
ROPE_BASE = 10000.0
CHUNK = 64
GLA_H, GLA_DK, GLA_DV, GLA_RANK, GLA_TAU = 4, 64, 128, 16, 16.0
ML_H, ML_DH = 4, 128
NA_H, NA_DH, NA_WIN_H, NA_WIN_W = 16, 64, 8, 16
D_FF = 2816
DEPTH = 4
N_MOD_ROWS = 16

TM = 1024
SUB = 512
TMF = 512
SB = 256
HALO = 16
FC = 256
EVEN_N = 3712
SMALL_COL = 3584
VMEM_LIMIT = 58 * 1024 * 1024


def _dot(a, b):
    return jnp.dot(a, b, preferred_element_type=F32)


def _dot_nt(a, b):
    return lax.dot_general(a, b, (((1,), (1,)), ((), ())), preferred_element_type=F32)


def _dot_tn(a, b):
    return lax.dot_general(a, b, (((0,), (0,)), ((), ())), preferred_element_type=F32)


def _split_bf16(x):
    hi = x.astype(BF16)
    lo = (x - hi.astype(F32)).astype(BF16)
    return hi, lo


def _log_sigmoid(x):
    return jnp.minimum(x, 0.0) - jnp.log(1.0 + jnp.exp(-jnp.abs(x)))


def _sigmoid(x):
    return 1.0 / (1.0 + jnp.exp(-x))


def _params(n_axes=1):
    return pltpu.CompilerParams(dimension_semantics=("arbitrary",) * n_axes, vmem_limit_bytes=VMEM_LIMIT)


def _resident(shape):
    nd = len(shape)
    return pl.BlockSpec(shape, lambda *_: (0,) * nd, pipeline_mode=pl.Buffered(1))


def _mod_kernel(cv_ref, w_ref, b_ref, o_ref):
    a = cv_ref[...]
    a = a * _sigmoid(a)
    w = w_ref[0]
    a_hi, a_lo = _split_bf16(a)
    w_hi, w_lo = _split_bf16(w)
    o_ref[0] = _dot(a_hi, w_hi) + _dot(a_hi, w_lo) + _dot(a_lo, w_hi) + b_ref[0]


def _modulation(cvec, mod_w, mod_b):
    tn = 1536
    return pl.pallas_call(
        _mod_kernel,
        grid=(DEPTH, 6 * D // tn),
        in_specs=[
            pl.BlockSpec((N_MOD_ROWS, D), lambda l, j: (0, 0)),
            pl.BlockSpec((1, D, tn), lambda l, j: (l, 0, j)),
            pl.BlockSpec((1, 1, tn), lambda l, j: (l, 0, j)),
        ],
        out_specs=pl.BlockSpec((1, N_MOD_ROWS, tn), lambda l, j: (l, 0, j)),
        out_shape=jax.ShapeDtypeStruct((DEPTH, N_MOD_ROWS, 6 * D), F32),
        compiler_params=_params(2),
        name="adaln_table",
    )(cvec, mod_w, mod_b.reshape(DEPTH, 1, 6 * D))


def _norm_mod(x, mult, shift):
    ms = jnp.mean(x * x, axis=-1, keepdims=True)
    return x * lax.rsqrt(ms + EPS) * mult + shift


def _stream_operand(H, n_lat_tiles, tm=TM):
    if isinstance(H, tuple):
        specs = [pl.BlockSpec((tm, D), lambda i: (jnp.minimum(i, n_lat_tiles - 1), 0)),
                 pl.BlockSpec((tm, D), lambda i: (jnp.maximum(i - n_lat_tiles, 0), 0))]
        return specs, list(H), sum(h.shape[0] for h in H)
    return [pl.BlockSpec((tm, D), lambda i: (i, 0))], [H], H.shape[0]


def _stream_rows(h_refs, rows, n_lat_tiles):
    if len(h_refs) == 2:
        return jnp.where(pl.program_id(0) < n_lat_tiles, h_refs[0][rows, :], h_refs[1][rows, :])
    return h_refs[0][rows, :]


def _mod_row_map(n_lat_tiles, tiles_per_seq, ctx_row):
    def index_map(i):
        return (jnp.where(i < n_lat_tiles, i // tiles_per_seq, ctx_row), 0, 0)
    return index_map


def _rope_gla(x, c, s):
    lane = lax.broadcasted_iota(jnp.int32, x.shape, 1)
    swapped = jnp.where((lane % GLA_DK) < GLA_DK // 2, pltpu.roll(x, 128 - GLA_DK // 2, 1),
                        pltpu.roll(x, GLA_DK // 2, 1))
    return x * c + swapped * s


def _rope_ml(x, c, s):
    return x * c + pltpu.roll(x, ML_DH // 2, 1) * s


def _even_proj_kernel(*refs, n_lat_tiles, tm):
    h_refs = refs[:-8]
    mult_ref, shift_ref, w_ref, cg_ref, sg_ref, cm_ref, sm_ref, z_ref = refs[-8:]
    q_scale = {0: GLA_DK ** -0.5, 1: 1.0, 6: 1.0, 7: 1.0, 8: ML_DH ** -0.5, 9: ML_DH ** -0.5}
    for r0 in range(0, tm, SUB):
        rows = slice(r0, r0 + SUB)
        u = _norm_mod(_stream_rows(h_refs, rows, n_lat_tiles), mult_ref[0], shift_ref[0]).astype(BF16)
        for j in range(SMALL_COL // 256):
            z = _dot(u, w_ref[:, j * 256:(j + 1) * 256])
            if j in q_scale:
                rope = _rope_gla if j < 2 else _rope_ml
                c_ref, s_ref = (cg_ref, sg_ref) if j < 2 else (cm_ref, sm_ref)
                for half in range(2):
                    zz = rope(z[:, half * 128:(half + 1) * 128], c_ref[rows, :], s_ref[rows, :]) * q_scale[j]
                    z_ref[rows, j * 256 + half * 128:j * 256 + (half + 1) * 128] = zz
            else:
                z_ref[rows, j * 256:(j + 1) * 256] = z
        z_ref[rows, SMALL_COL:EVEN_N] = _dot(u, w_ref[:, SMALL_COL:EVEN_N])


def _even_proj(H, mult, shift, w, ropes, n_lat_rows, T, ctx_row):
    tm = SUB if isinstance(H, tuple) else TM
    n_lat_tiles, tiles_per_seq = n_lat_rows // tm, T // tm
    h_specs, h_args, R = _stream_operand(H, n_lat_tiles, tm)
    rowmap = _mod_row_map(n_lat_tiles, tiles_per_seq, ctx_row)
    ropemap = lambda i: (jnp.where(i < n_lat_tiles, i % tiles_per_seq, tiles_per_seq), 0)
    return pl.pallas_call(
        functools.partial(_even_proj_kernel, n_lat_tiles=n_lat_tiles, tm=tm),
        grid=(R // tm,),
        in_specs=h_specs + [
            pl.BlockSpec((1, 1, D), rowmap),
            pl.BlockSpec((1, 1, D), rowmap),
            _resident((D, EVEN_N)),
        ] + [pl.BlockSpec((tm, 128), ropemap)] * 4,
        out_specs=pl.BlockSpec((tm, EVEN_N), lambda i: (i, 0)),
        out_shape=jax.ShapeDtypeStruct((R, EVEN_N), F32),
        compiler_params=_params(),
        name="even_in_proj",
    )(*h_args, mult, shift, w, *ropes)


def _chunk_cummax(x, z, tmod):
    n = x.shape[0]
    k = 1
    while k < CHUNK:
        if z == 0:
            shifted, valid = pltpu.roll(x, k, 0), tmod >= k
        else:
            shifted, valid = pltpu.roll(x, n - k, 0), tmod < CHUNK - k
        x = jnp.maximum(x, jnp.where(valid, shifted, -jnp.inf))
        k *= 2
    return x


def _scan_kernel(*args):
    in_refs = (args[0:7], args[7:14])
    aup_ref, abias_ref, mgb_ref, of_ref, ob_ref, s_ref, c_ref, m_ref = args[14:]
    out_refs = (of_ref, ob_ref)
    dirs = (0, 1)
    n_chunks = SB // CHUNK
    order = (tuple(range(n_chunks)), tuple(reversed(range(n_chunks))))
    edge = lambda z, c: c * CHUNK + (CHUNK - 1 if z == 0 else 0)
    rows_of = lambda c: slice(c * CHUNK, (c + 1) * CHUNK)

    @pl.when(pl.program_id(1) == 0)
    def _():
        s_ref[...] = jnp.zeros_like(s_ref)
        c_ref[...] = jnp.zeros_like(c_ref)
        m_ref[...] = jnp.zeros_like(m_ref)

    ti = lax.broadcasted_iota(jnp.int32, (SB, SB), 0)
    si = lax.broadcasted_iota(jnp.int32, (SB, SB), 1)
    same = (ti // CHUNK) == (si // CHUNK)
    mask = [same & (si <= ti), same & (si >= ti)]
    tri = [x.astype(BF16) for x in mask]

    def chunk_totals(run, z):
        return jnp.concatenate([jnp.broadcast_to(run[edge(z, c):edge(z, c) + 1, :], (CHUNK, run.shape[1]))
                                for c in range(n_chunks)], axis=0)

    tmod = lax.broadcasted_iota(jnp.int32, (SB, 128), 0) % CHUNK
    lane256 = lax.broadcasted_iota(jnp.int32, (SB, GLA_H * GLA_DK), 1)
    in_head = [(lane256 // GLA_DK) == h for h in range(GLA_H)]
    lane128 = lax.broadcasted_iota(jnp.int32, (1, 128), 1)
    gate_col = lambda z, h: 2 * GLA_RANK + z * 2 * ML_H + h
    ones_b = jnp.ones((SB, ML_DH), BF16)

    P = []
    for z in dirs:
        gq_ref, gk_ref, gv_ref, sm_ref, mq_ref, mk_ref, mv_ref = in_refs[z]
        small = sm_ref[...]
        a_pre = _dot(small.astype(BF16), aup_ref[z]) + abias_ref[z]
        la_hi, la_lo = _split_bf16(_log_sigmoid(a_pre) * (1.0 / GLA_TAU))
        b = _dot(tri[z], la_hi) + _dot(tri[z], la_lo)
        gb = chunk_totals(b, z)
        q = gq_ref[...]
        k = gk_ref[...]
        gates = small + mgb_ref[...]
        lf_hi, lf_lo = _split_bf16(_log_sigmoid(gates))
        bc = pltpu.roll(_dot(tri[z], lf_hi) + _dot(tri[z], lf_lo), 128 - ML_H, 1)
        gc = chunk_totals(bc, z)
        w = gates - bc
        cm = _chunk_cummax(w, z, tmod)
        P.append(dict(
            q_dec=(q * jnp.exp(b)).astype(BF16), k_dec=(k * jnp.exp(-b)).astype(BF16),
            k_end=(k * jnp.exp(gb - b)).astype(BF16), gb=gb, bc=bc, gc=gc, w=w, cm=cm, w_t=w.T))

    G, M = {}, {}
    for z in dirs:
        gq_ref, gk_ref, gv_ref, sm_ref, mq_ref, mk_ref, mv_ref = in_refs[z]
        p = P[z]
        for h in range(GLA_H):
            qm = jnp.where(in_head[h], p["q_dec"], jnp.zeros_like(p["q_dec"]))
            km = jnp.where(in_head[h], p["k_end"], jnp.zeros_like(p["k_end"]))
            v = gv_ref[:, h * GLA_DV:(h + 1) * GLA_DV].astype(BF16)
            G[z, h] = dict(qm=qm, km=km, v=v, sc=_dot_nt(qm, p["k_dec"]))
        for h in range(ML_H):
            fi = gate_col(z, h)
            cols = slice(h * ML_DH, (h + 1) * ML_DH)
            d = jnp.exp(jnp.where(mask[z], p["w_t"][fi:fi + 1, :] - p["cm"][:, fi:fi + 1], -jnp.inf))
            qb = mq_ref[:, cols].astype(BF16)
            va = jnp.concatenate([mv_ref[:, cols].astype(BF16), ones_b], axis=1)
            M[z, h] = dict(qb=qb, va=va, d=d, qk=_dot_nt(qb, mk_ref[:, cols].astype(BF16)))
    for z in dirs:
        for h in range(GLA_H):
            e = G[z, h]
            e["intra"] = _dot(jnp.where(mask[z], e["sc"], 0.0).astype(BF16), e["v"])
        for h in range(ML_H):
            e = M[z, h]
            e["intra"] = _dot((e["qk"] * e["d"]).astype(BF16), e["va"])
    for z in dirs:
        mk_ref = in_refs[z][5]
        p = P[z]
        for c in range(n_chunks):
            r = rows_of(c)
            wmax = p["cm"][edge(z, c):edge(z, c) + 1, :]
            w_end = jnp.exp(p["w"][r] - wmax)
            p["m_loc", c] = wmax + p["gc"][c * CHUNK:c * CHUNK + 1, :]
            for h in range(GLA_H):
                e = G[z, h]
                e["kv", c] = _dot_tn(e["v"][r], e["km"][r])
            for h in range(ML_H):
                e = M[z, h]
                fi = gate_col(z, h)
                kw = (mk_ref[r, h * ML_DH:(h + 1) * ML_DH] * w_end[:, fi:fi + 1]).astype(BF16)
                e["cl", c] = _dot_tn(kw, e["va"][r])

    S = {(z, h): s_ref[z, h] for z in dirs for h in range(GLA_H)}
    C = {(z, h): c_ref[z, h] for z in dirs for h in range(ML_H)}
    m_old = [m_ref[z] for z in dirs]
    head_lanes = [functools.reduce(jnp.logical_or, [lane128 == gate_col(z, h) for h in range(ML_H)]) for z in dirs]
    for step in range(n_chunks):
        g_inter, m_inter = {}, {}
        for z in dirs:
            r = rows_of(order[z][step])
            for h in range(GLA_H):
                g_inter[z, h] = _dot_nt(G[z, h]["qm"][r], S[z, h].astype(BF16))
            for h in range(ML_H):
                m_inter[z, h] = _dot(M[z, h]["qb"][r], C[z, h].astype(BF16))
        for z in dirs:
            c = order[z][step]
            r = rows_of(c)
            p = P[z]
            decay = jnp.exp(p["gb"][c * CHUNK:c * CHUNK + 1, :])
            for h in range(GLA_H):
                o = G[z, h]["intra"][r] + g_inter[z, h]
                out_refs[z][r, h * GLA_DV:(h + 1) * GLA_DV] = o.astype(out_refs[z].dtype)
                S[z, h] = decay * S[z, h] + G[z, h]["kv", c]
            mx = jnp.maximum(m_old[z], p["cm"][r])
            alpha = jnp.exp(p["cm"][r] - mx)
            inter_w = jnp.exp(m_old[z] - mx)
            floor = jnp.exp(-(p["bc"][r] + mx))
            g_c = p["gc"][c * CHUNK:c * CHUNK + 1, :]
            m_new = jnp.where(head_lanes[z], jnp.maximum(g_c + m_old[z], p["m_loc", c]), 0.0)
            a_sc = jnp.exp(g_c + m_old[z] - m_new)
            w_sc = jnp.exp(p["m_loc", c] - m_new)
            for h in range(ML_H):
                fi = gate_col(z, h)
                al, iw = alpha[:, fi:fi + 1], inter_w[:, fi:fi + 1]
                tot = al * M[z, h]["intra"][r] + iw * m_inter[z, h]
                den = jnp.maximum(jnp.abs(tot[:, ML_DH:]), floor[:, fi:fi + 1])
                col = GLA_H * GLA_DV + h * ML_DH
                out_refs[z][r, col:col + ML_DH] = (tot[:, :ML_DH] / den).astype(out_refs[z].dtype)
                C[z, h] = a_sc[:, fi:fi + 1] * C[z, h] + w_sc[:, fi:fi + 1] * M[z, h]["cl", c]
            m_old[z] = m_new
    for z in dirs:
        for h in range(GLA_H):
            s_ref[z, h] = S[z, h]
        for h in range(ML_H):
            c_ref[z, h] = C[z, h]
        m_ref[z] = m_old[z]


def _scan(Z, aup, abias, mgb, B, T, LC):
    R = Z.shape[0]
    assert LC == SB and T % SB == 0
    nb = T // SB
    ctx0 = B * nb
    fmap = lambda b, s: jnp.where(s == 0, ctx0 + b, b * nb + s - 1)
    bmap = lambda b, s: jnp.where(s == 0, ctx0 + b, b * nb + nb - s)

    def specs(rowmap):
        col = lambda width, blk: pl.BlockSpec((SB, width), lambda b, s: (rowmap(b, s), blk))
        return [col(256, 0), col(256, 1), col(512, 1), col(128, SMALL_COL // 128),
                col(512, 3), col(512, 4), col(512, 5)]

    out_spec = lambda rowmap: pl.BlockSpec((SB, D), lambda b, s: (rowmap(b, s), 0))
    return pl.pallas_call(
        _scan_kernel,
        grid=(B, nb + 1),
        in_specs=specs(fmap) + specs(bmap) + [
            pl.BlockSpec((2, 128, 256), lambda b, s: (0, 0, 0)),
            pl.BlockSpec((2, 1, 256), lambda b, s: (0, 0, 0)),
            pl.BlockSpec((1, 128), lambda b, s: (0, 0)),
        ],
        out_specs=[out_spec(fmap), out_spec(bmap)],
        out_shape=[jax.ShapeDtypeStruct((R, D), BF16)] * 2,
        scratch_shapes=[
            pltpu.VMEM((2, GLA_H, GLA_DV, GLA_H * GLA_DK), F32),
            pltpu.VMEM((2, ML_H, ML_DH, 2 * ML_DH), F32),
            pltpu.VMEM((2, 1, 128), F32),
        ],
        compiler_params=_params(2),
        name="gla_mlstm_scan",
    )(*([Z] * 14), aup, abias, mgb)


def _even_out_kernel(*refs, n_lat_tiles):
    of_ref, ob_ref, gg_ref, mo_ref = refs[:4]
    h_refs = refs[4:-6]
    gate_ref, gnw_ref, mnw_ref, w_ref, o_ref, y_ref = refs[-6:]
    for r0 in range(0, TM, SUB):
        rows = slice(r0, r0 + SUB)
        for hh in range(GLA_H + ML_H):
            cols = slice(hh * 128, (hh + 1) * 128)
            x = of_ref[rows, cols].astype(F32) + ob_ref[rows, cols].astype(F32)
            ms = jnp.mean(x * x, axis=-1, keepdims=True)
            if hh < GLA_H:
                gt = gg_ref[rows, cols]
                y = x * lax.rsqrt(ms + EPS) * gnw_ref[...] * (gt * _sigmoid(gt))
            else:
                gt = mo_ref[rows, (hh - GLA_H) * 128:(hh - GLA_H + 1) * 128]
                y = x * lax.rsqrt(ms + EPS) * mnw_ref[...] * _sigmoid(gt)
            y_ref[rows, cols] = y.astype(BF16)
        o_ref[rows, :] = _stream_rows(h_refs, rows, n_lat_tiles) + gate_ref[0] * _dot(y_ref[rows, :], w_ref[...])


def _even_out(of, ob, Z, H, gate, gnw, mnw, w, n_lat_tiles, tiles_per_seq, ctx_row):
    h_specs, h_args, R = _stream_operand(H, n_lat_tiles)
    rowmap = _mod_row_map(n_lat_tiles, tiles_per_seq, ctx_row)
    return pl.pallas_call(
        functools.partial(_even_out_kernel, n_lat_tiles=n_lat_tiles),
        grid=(R // TM,),
        in_specs=[
            pl.BlockSpec((TM, D), lambda i: (i, 0)),
            pl.BlockSpec((TM, D), lambda i: (i, 0)),
            pl.BlockSpec((TM, 512), lambda i: (i, 2)),
            pl.BlockSpec((TM, 512), lambda i: (i, 6)),
        ] + h_specs + [
            pl.BlockSpec((1, 1, D), rowmap),
            pl.BlockSpec((1, 128), lambda i: (0, 0)),
            pl.BlockSpec((1, 128), lambda i: (0, 0)),
            _resident((D, D)),
        ],
        out_specs=pl.BlockSpec((TM, D), lambda i: (i, 0)),
        out_shape=jax.ShapeDtypeStruct((R, D), F32),
        scratch_shapes=[pltpu.VMEM((TM, D), BF16)],
        input_output_aliases={4: 0} if len(h_args) == 1 else {},
        compiler_params=_params(),
        name="even_out_proj",
    )(of, ob, Z, Z, *h_args, gate, gnw, mnw, w)


def _na_proj_kernel(h_ref, mult_ref, shift_ref, w_ref, z_ref):
    for r0 in range(0, TM, SUB):
        rows = slice(r0, r0 + SUB)
        u = _norm_mod(h_ref[rows, :], mult_ref[0], shift_ref[0]).astype(BF16)
        for j in range(3 * D // 256):
            z = _dot(u, w_ref[:, j * 256:(j + 1) * 256])
            if j < D // 256:
                z = z * (NA_DH ** -0.5)
            z_ref[rows, j * 256:(j + 1) * 256] = z.astype(BF16)


def _na_proj(H, mult, shift, w, n_lat_tiles, tiles_per_seq, ctx_row):
    R = H.shape[0]
    rowmap = _mod_row_map(n_lat_tiles, tiles_per_seq, ctx_row)
    return pl.pallas_call(
        _na_proj_kernel,
        grid=(R // TM,),
        in_specs=[
            pl.BlockSpec((TM, D), lambda i: (i, 0)),
            pl.BlockSpec((1, 1, D), rowmap),
            pl.BlockSpec((1, 1, D), rowmap),
            _resident((D, 3 * D)),
        ],
        out_specs=pl.BlockSpec((TM, 3 * D), lambda i: (i, 0)),
        out_shape=jax.ShapeDtypeStruct((R, 3 * D), BF16),
        compiler_params=_params(),
        name="na_qkv_proj",
    )(H, mult, shift, w)


NA_ROWS_PER_STEP = 4


def _stack_heads(q):
    lane = lax.broadcasted_iota(jnp.int32, q.shape, 1)
    zero = jnp.zeros_like(q)
    return jnp.concatenate([jnp.where(lane < NA_DH, q, zero), jnp.where(lane >= NA_DH, q, zero)], axis=0)


def _softmax_pv(s_list, v_list):
    m = functools.reduce(jnp.maximum, [jnp.max(s, axis=1, keepdims=True) for s in s_list])
    p_list = [jnp.exp(s - m) for s in s_list]
    l = functools.reduce(jnp.add, [jnp.sum(p, axis=1, keepdims=True) for p in p_list])
    o = functools.reduce(jnp.add, [_dot(p.astype(BF16), v) for p, v in zip(p_list, v_list)])
    return o / l


def _unstack_heads(o):
    n = o.shape[0] // 2
    lane = lax.broadcasted_iota(jnp.int32, (n, 128), 1)
    return jnp.where(lane < NA_DH, o[:n], o[n:])


def _na_kernel(q_ref, k_ref, v_ref, kc_ref, vc_ref, qc_ref, tb_ref, o_ref, oc_ref, sw_ref, sc_ref, *, rows):
    kc = kc_ref[...]
    vc = vc_ref[...]
    win = NA_WIN_H * GRID_W
    n2 = 2 * GRID_W
    rps = NA_ROWS_PER_STEP
    n_groups = rows // rps

    def group_rows(g):
        rs = [g * rps + u for u in range(rps)]
        kr0 = [jnp.clip(r - NA_WIN_H // 2, 0, rows - NA_WIN_H) for r in rs]
        qrows = [pl.ds(pl.multiple_of(r * GRID_W, GRID_W), GRID_W) for r in rs]
        krows = [pl.ds(pl.multiple_of(k0 * GRID_W, GRID_W), win) for k0 in kr0]
        return rs, kr0, qrows, krows

    def scores(g, slot):
        rs, kr0, qrows, krows = group_rows(g)
        q2 = [_stack_heads(q_ref[qr, :]) for qr in qrows]
        s_w = [_dot_nt(q, k_ref[kr, :]) for q, kr in zip(q2, krows)]
        sc_ref[slot] = _dot_nt(jnp.concatenate(q2, axis=0), kc)
        for u, (s, r, k0) in enumerate(zip(s_w, rs, kr0)):
            bias = [tb_ref[0, j + NA_WIN_H - 1 - (r - k0)] for j in range(0, NA_WIN_H, 2)]
            sw_ref[slot, u] = s + jnp.concatenate(bias, axis=1)

    def softmax_pv(g, slot):
        _, _, qrows, krows = group_rows(g)
        s_w = [sw_ref[slot, u] for u in range(rps)]
        s_c = [sc_ref[slot, u * n2:(u + 1) * n2, :] for u in range(rps)]
        m = [jnp.maximum(jnp.max(a, axis=1, keepdims=True), jnp.max(b, axis=1, keepdims=True))
             for a, b in zip(s_w, s_c)]
        p_w = [jnp.exp(s - mm) for s, mm in zip(s_w, m)]
        p_c = [jnp.exp(s - mm) for s, mm in zip(s_c, m)]
        l = [jnp.sum(a, axis=1, keepdims=True) + jnp.sum(b, axis=1, keepdims=True) for a, b in zip(p_w, p_c)]
        o_c_all = _dot(jnp.concatenate([b.astype(BF16) for b in p_c], axis=0), vc)
        o = [_dot(a.astype(BF16), v_ref[kr, :]) + o_c_all[u * n2:(u + 1) * n2]
             for u, (a, kr) in enumerate(zip(p_w, krows))]
        for oo, ll, qr in zip(o, l, qrows):
            o_ref[qr, :] = _unstack_heads(oo / ll).astype(o_ref.dtype)

    scores(0, 0)

    def pair_body(i, carry):
        g = 2 * i
        scores(g + 1, 1)
        softmax_pv(g, 0)
        scores(jnp.minimum(g + 2, n_groups - 1), 0)
        softmax_pv(g + 1, 1)
        return carry

    lax.fori_loop(0, n_groups // 2, pair_body, 0)

    oc = _softmax_pv([_dot_nt(_stack_heads(qc_ref[...]), kc)], [vc])
    oc_ref[...] = _unstack_heads(oc).astype(oc_ref.dtype)


def _na_attention(Zq, tb, B, T, LC):
    rows = T // GRID_W
    assert rows % (2 * NA_ROWS_PER_STEP) == 0 and rows >= NA_WIN_H
    n_pairs = NA_H // 2
    score_slots = [pltpu.VMEM((2, NA_ROWS_PER_STEP, 2 * GRID_W, NA_WIN_H * GRID_W), F32),
                   pltpu.VMEM((2, NA_ROWS_PER_STEP * 2 * GRID_W, LC), F32)]
    ctx_blk0 = B * T // LC
    lat = lambda part: pl.BlockSpec((T, 128), lambda b, p: (b, part * n_pairs + p))
    ctx = lambda part: pl.BlockSpec((LC, 128), lambda b, p: (ctx_blk0 + b, part * n_pairs + p))
    return pl.pallas_call(
        functools.partial(_na_kernel, rows=rows),
        grid=(B, n_pairs),
        in_specs=[lat(0), lat(1), lat(2), ctx(1), ctx(2), ctx(0),
                  pl.BlockSpec((1, 2 * NA_WIN_H - 2, 2 * GRID_W, 2 * GRID_W), lambda b, p: (p, 0, 0, 0))],
        out_specs=[pl.BlockSpec((T, 128), lambda b, p: (b, p)),
                   pl.BlockSpec((LC, 128), lambda b, p: (b, p))],
        out_shape=[jax.ShapeDtypeStruct((B * T, D), BF16), jax.ShapeDtypeStruct((B * LC, D), BF16)],
        scratch_shapes=score_slots,
        compiler_params=_params(2),
        name="na_attention",
    )(Zq, Zq, Zq, Zq, Zq, Zq, tb)


def _na_bias_tables(rpb):
    c, kc = np.meshgrid(np.arange(GRID_W), np.arange(GRID_W), indexing="ij")
    start = np.clip(c - NA_WIN_W // 2, 0, GRID_W - NA_WIN_W)
    inside = (kc >= start) & (kc < start + NA_WIN_W)
    onehot = np.zeros((2 * NA_WIN_W - 1, GRID_W, GRID_W), np.float32)
    onehot[(kc - c + NA_WIN_W - 1)[inside], c[inside], kc[inside]] = 1.0
    toep = jnp.einsum("hdi,ick->hdck", rpb.astype(F32), onehot, precision=lax.Precision.HIGHEST)
    toep = jnp.where(inside[None, None], toep, -1e30)
    two = jnp.concatenate([toep[:, :-1], toep[:, 1:]], axis=-1)
    n_d = 2 * NA_WIN_H - 2
    return (two.reshape(NA_H // 2, 2, n_d, GRID_W, 2 * GRID_W).transpose(0, 2, 1, 3, 4)
            .reshape(NA_H // 2, n_d, 2 * GRID_W, 2 * GRID_W))


def _na_out_kernel(yl_ref, yc_ref, h_ref, gate_ref, w_ref, o_ref, *, n_lat_tiles):
    y = jnp.where(pl.program_id(0) < n_lat_tiles, yl_ref[...], yc_ref[...])
    o_ref[...] = h_ref[...] + gate_ref[0] * _dot(y, w_ref[...])


def _na_out(Y_lat, Y_ctx, H, gate, w, n_lat_tiles, tiles_per_seq, ctx_row):
    R = H.shape[0]
    rowmap = _mod_row_map(n_lat_tiles, tiles_per_seq, ctx_row)
    return pl.pallas_call(
        functools.partial(_na_out_kernel, n_lat_tiles=n_lat_tiles),
        grid=(R // TM,),
        in_specs=[
            pl.BlockSpec((TM, D), lambda i: (jnp.minimum(i, n_lat_tiles - 1), 0)),
            pl.BlockSpec((TM, D), lambda i: (jnp.maximum(i - n_lat_tiles, 0), 0)),
            pl.BlockSpec((TM, D), lambda i: (i, 0)),
            pl.BlockSpec((1, 1, D), rowmap),
            _resident((D, D)),
        ],
        out_specs=pl.BlockSpec((TM, D), lambda i: (i, 0)),
        out_shape=jax.ShapeDtypeStruct((R, D), F32),
        input_output_aliases={2: 0},
        compiler_params=_params(),
        name="na_out_proj",
    )(Y_lat, Y_ctx, H, gate, w)


def _ffn_kernel(hp_ref, h_ref, hn_ref, mult_ref, shift_ref, gate_ref, wup_ref, cw_ref, cb_ref, wd_ref, fnw_ref,
                o_ref, u_ref, act_ref, *, final, n_lat_rows, T, LC):
    x = h_ref[...]
    mult = mult_ref[0]
    shift = shift_ref[0]
    u_ref[0:HALO, :] = _norm_mod(hp_ref[...], mult, shift).astype(BF16)
    u_ref[HALO:HALO + TMF, :] = _norm_mod(x, mult, shift).astype(BF16)
    u_ref[HALO + TMF:, :] = _norm_mod(hn_ref[...], mult, shift).astype(BF16)

    r = pl.program_id(0) * TMF + lax.broadcasted_iota(jnp.int32, (TMF, 1), 0)
    is_lat = r < n_lat_rows
    pos = jnp.where(is_lat, r & (T - 1), r & (LC - 1))
    first = pos == 0
    last = pos == jnp.where(is_lat, T - 1, LC - 1)

    n_ext = TMF + 2 * HALO
    for c in range(D_FF // FC):
        cols = slice(c * FC, (c + 1) * FC)
        a = _dot(u_ref[...], wup_ref[:, cols])
        a_prev = pltpu.roll(a, 1, 0)[HALO:HALO + TMF]
        a_next = pltpu.roll(a, n_ext - 1, 0)[HALO:HALO + TMF]
        a_mid = a[HALO:HALO + TMF]
        conv = (cw_ref[0:1, cols] * jnp.where(first, 0.0, a_prev) + cw_ref[1:2, cols] * a_mid
                + cw_ref[2:3, cols] * jnp.where(last, 0.0, a_next) + cb_ref[:, cols])
        gt = _dot(u_ref[HALO:HALO + TMF, :], wup_ref[:, D_FF + c * FC:D_FF + (c + 1) * FC])
        act_ref[:, cols] = (jax.nn.gelu(conv, approximate=True) * gt).astype(BF16)

    y = x + gate_ref[0] * _dot(act_ref[...], wd_ref[...])
    if final:
        y = _norm_mod(y, fnw_ref[...], 0.0)
    o_ref[...] = y


def _ffn(H, mult, shift, gate, wup, cw, cb, wd, fnw, *, final, n_rows_out, ctx_row, n_lat_rows, T, LC):
    R = H.shape[0]
    rowmap = _mod_row_map(n_lat_rows // TMF, T // TMF, ctx_row)
    per = TMF // HALO
    last_halo_blk = R // HALO - 1
    kern = functools.partial(_ffn_kernel, final=final, n_lat_rows=n_lat_rows, T=T, LC=LC)
    return pl.pallas_call(
        kern,
        grid=(n_rows_out // TMF,),
        in_specs=[
            pl.BlockSpec((HALO, D), lambda i: (jnp.maximum(i * per - 1, 0), 0)),
            pl.BlockSpec((TMF, D), lambda i: (i, 0)),
            pl.BlockSpec((HALO, D), lambda i: (jnp.minimum((i + 1) * per, last_halo_blk), 0)),
            pl.BlockSpec((1, 1, D), rowmap),
            pl.BlockSpec((1, 1, D), rowmap),
            pl.BlockSpec((1, 1, D), rowmap),
            _resident((D, 2 * D_FF)),
            pl.BlockSpec((3, D_FF), lambda i: (0, 0)),
            pl.BlockSpec((1, D_FF), lambda i: (0, 0)),
            _resident((D_FF, D)),
            pl.BlockSpec((1, D), lambda i: (0, 0)),
        ],
        out_specs=pl.BlockSpec((TMF, D), lambda i: (i, 0)),
        out_shape=jax.ShapeDtypeStruct((n_rows_out, D), F32),
        scratch_shapes=[pltpu.VMEM((TMF + 2 * HALO, D), BF16), pltpu.VMEM((TMF, D_FF), BF16)],
        compiler_params=_params(),
        name="conv_ffn_final" if final else "conv_ffn",
    )(H, H, H, mult, shift, gate, wup, cw, cb, wd, fnw)


def _rope_tables(T, head_dim):
    pos = jnp.arange(T, dtype=jnp.int32)
    row = (pos // GRID_W).astype(F32)
    col = (pos % GRID_W).astype(F32)
    n_freq = head_dim // 4
    inv_freq = ROPE_BASE ** (-jnp.arange(n_freq, dtype=F32) / n_freq)
    ang = jnp.concatenate([row[:, None] * inv_freq, col[:, None] * inv_freq], axis=-1)
    cos, sin = jnp.cos(ang), jnp.sin(ang)
    reps = 128 // head_dim
    c = jnp.tile(jnp.concatenate([cos, cos], axis=-1), (1, reps))
    s = jnp.tile(jnp.concatenate([-sin, sin], axis=-1), (1, reps))
    c = jnp.concatenate([c, jnp.ones((TM, 128), F32)], axis=0)
    s = jnp.concatenate([s, jnp.zeros((TM, 128), F32)], axis=0)
    return c, s


def kernel(x, c, ctx, c_ctx, mod_w, mod_b, norm_mix_w, norm_ffn_w, ev_w_in, ev_w_out, gla_a_up, gla_a_bias,
           gla_norm_w, ml_gate_bias, ml_norm_w, na_w_qkv, na_w_out, na_rpb, ffn_w_up, ffn_conv_w, ffn_conv_b,
           ffn_w_down, final_norm_w):
    B, T, _ = x.shape
    LC = ctx.shape[1]
    assert x.shape[2] == D and T % TM == 0 and (B * LC) % TM == 0 and B < N_MOD_ROWS
    assert T & (T - 1) == 0 and LC & (LC - 1) == 0 and T % GRID_W == 0
    n_lat_rows = B * T
    n_lat_tiles = n_lat_rows // TM
    tiles_per_seq = T // TM
    tile_kw = dict(n_lat_tiles=n_lat_tiles, tiles_per_seq=tiles_per_seq, ctx_row=B)

    H = (x.reshape(B * T, D), ctx.reshape(B * LC, D))
    R = n_lat_rows + B * LC

    cvec = jnp.zeros((N_MOD_ROWS, D), F32).at[:B].set(c).at[B].set(c_ctx)
    mods = _modulation(cvec, mod_w, mod_b)
    ropes = _rope_tables(T, GLA_DK) + _rope_tables(T, ML_DH)

    out = None
    for layer in range(DEPTH):
        j = layer // 2
        last = layer == DEPTH - 1
        sh1, sc1, g1, sh2, sc2, g2 = [m.reshape(N_MOD_ROWS, 1, D) for m in jnp.split(mods[layer], 6, axis=-1)]
        mult1 = norm_mix_w[layer] * (1.0 + sc1)
        mult2 = norm_ffn_w[layer] * (1.0 + sc2)
        if layer % 2 == 0:
            w = ev_w_in[j]
            w_re = jnp.concatenate(
                [w[:, 0:1536], w[:, 1568:3616], w[:, 1536:1568], w[:, 3616:3632],
                 jnp.zeros((D, EVEN_N - 3632), F32)], axis=1).astype(BF16)
            Z = _even_proj(H, mult1, sh1, w_re, ropes, n_lat_rows, T, B)
            aup = jnp.zeros((2, 128, GLA_H * GLA_DK), F32)
            aup = aup.at[0, 0:GLA_RANK].set(gla_a_up[j, 0]).at[1, GLA_RANK:2 * GLA_RANK].set(gla_a_up[j, 1])
            mgb = jnp.zeros((1, 128), F32).at[0, 2 * GLA_RANK:2 * GLA_RANK + 4 * ML_H].set(ml_gate_bias[j].reshape(-1))
            of, ob = _scan(Z, aup.astype(BF16), gla_a_bias[j].reshape(2, 1, -1), mgb, B, T, LC)
            H = _even_out(of, ob, Z, H, g1, gla_norm_w[j].reshape(1, -1), ml_norm_w[j].reshape(1, -1),
                          ev_w_out[j].astype(BF16), **tile_kw)
        else:
            Zq = _na_proj(H, mult1, sh1, na_w_qkv[j].astype(BF16), **tile_kw)
            o_lat, o_ctx = _na_attention(Zq, _na_bias_tables(na_rpb[j]), B, T, LC)
            H = _na_out(o_lat, o_ctx, H, g1, na_w_out[j].astype(BF16), **tile_kw)
        res = _ffn(H, mult2, sh2, g2, ffn_w_up[layer].astype(BF16), ffn_conv_w[layer],
                   ffn_conv_b[layer].reshape(1, -1), ffn_w_down[layer].astype(BF16), final_norm_w.reshape(1, -1),
                   final=last, n_rows_out=n_lat_rows if last else R, ctx_row=B, n_lat_rows=n_lat_rows, T=T, LC=LC)
        if last:
            out = res
        else:
            H = res
    return out.reshape(B, T, D)
```

```python
import functools

import numpy as np
import jax
import jax.numpy as jnp
from jax import lax
from jax.experimental import pallas as pl
from jax.experimental.pallas import tpu as pltpu

F32 = jnp.float32
BF16 = jnp.bfloat16

D = 1024
GRID_W = 64
EPS = 1e-6
ROPE_BASE = 10000.0
CHUNK = 64
GLA_H, GLA_DK, GLA_DV, GLA_RANK, GLA_TAU = 4, 64, 128, 16, 16.0
ML_H, ML_DH = 4, 128
NA_H, NA_DH, NA_WIN_H, NA_WIN_W = 16, 64, 8, 16
D_FF = 2816
DEPTH = 4
N_MOD_ROWS = 16

TM = 1024
SUB = 512
TMF = 512
SB = 256
HALO = 16
FC = 256
EVEN_N = 3712
SMALL_COL = 3584
VMEM_LIMIT = 58 * 1024 * 1024


def _dot(a, b):
    return jnp.dot(a, b, preferred_element_type=F32)


def _dot_nt(a, b):
    return lax.dot_general(a, b, (((1,), (1,)), ((), ())), preferred_element_type=F32)


def _dot_tn(a, b):
    return lax.dot_general(a, b, (((0,), (0,)), ((), ())), preferred_element_type=F32)


def _split_bf16(x):
    hi = x.astype(BF16)
    lo = (x - hi.astype(F32)).astype(BF16)
    return hi, lo


def _log_sigmoid(x):
    return jnp.minimum(x, 0.0) - jnp.log(1.0 + jnp.exp(-jnp.abs(x)))


def _sigmoid(x):
    return 1.0 / (1.0 + jnp.exp(-x))


def _params(n_axes=1):
    return pltpu.CompilerParams(dimension_semantics=("arbitrary",) * n_axes, vmem_limit_bytes=VMEM_LIMIT)


def _resident(shape):
    nd = len(shape)
    return pl.BlockSpec(shape, lambda *_: (0,) * nd, pipeline_mode=pl.Buffered(1))


def _mod_kernel(cv_ref, w_ref, b_ref, o_ref):
    a = cv_ref[...]
    a = a * _sigmoid(a)
    w = w_ref[0]
    a_hi, a_lo = _split_bf16(a)
    w_hi, w_lo = _split_bf16(w)
    o_ref[0] = _dot(a_hi, w_hi) + _dot(a_hi, w_lo) + _dot(a_lo, w_hi) + b_ref[0]


def _modulation(cvec, mod_w, mod_b):
    tn = 1536
    return pl.pallas_call(
        _mod_kernel,
        grid=(DEPTH, 6 * D // tn),
        in_specs=[
            pl.BlockSpec((N_MOD_ROWS, D), lambda l, j: (0, 0)),
            pl.BlockSpec((1, D, tn), lambda l, j: (l, 0, j)),
            pl.BlockSpec((1, 1, tn), lambda l, j: (l, 0, j)),
        ],
        out_specs=pl.BlockSpec((1, N_MOD_ROWS, tn), lambda l, j: (l, 0, j)),
        out_shape=jax.ShapeDtypeStruct((DEPTH, N_MOD_ROWS, 6 * D), F32),
        compiler_params=_params(2),
        name="adaln_table",
    )(cvec, mod_w, mod_b.reshape(DEPTH, 1, 6 * D))


def _norm_mod(x, mult, shift):
    ms = jnp.mean(x * x, axis=-1, keepdims=True)
    return x * lax.rsqrt(ms + EPS) * mult + shift


def _stream_operand(H, n_lat_tiles, tm=TM):
    if isinstance(H, tuple):
        specs = [pl.BlockSpec((tm, D), lambda i: (jnp.minimum(i, n_lat_tiles - 1), 0)),
                 pl.BlockSpec((tm, D), lambda i: (jnp.maximum(i - n_lat_tiles, 0), 0))]
        return specs, list(H), sum(h.shape[0] for h in H)
    return [pl.BlockSpec((tm, D), lambda i: (i, 0))], [H], H.shape[0]


def _stream_rows(h_refs, rows, n_lat_tiles):
    if len(h_refs) == 2:
        return jnp.where(pl.program_id(0) < n_lat_tiles, h_refs[0][rows, :], h_refs[1][rows, :])
    return h_refs[0][rows, :]


def _mod_row_map(n_lat_tiles, tiles_per_seq, ctx_row):
    def index_map(i):
        return (jnp.where(i < n_lat_tiles, i // tiles_per_seq, ctx_row), 0, 0)
    return index_map


def _rope_gla(x, c, s):
    lane = lax.broadcasted_iota(jnp.int32, x.shape, 1)
    swapped = jnp.where((lane % GLA_DK) < GLA_DK // 2, pltpu.roll(x, 128 - GLA_DK // 2, 1),
                        pltpu.roll(x, GLA_DK // 2, 1))
    return x * c + swapped * s


def _rope_ml(x, c, s):
    return x * c + pltpu.roll(x, ML_DH // 2, 1) * s


def _even_proj_kernel(*refs, n_lat_tiles, tm):
    h_refs = refs[:-8]
    mult_ref, shift_ref, w_ref, cg_ref, sg_ref, cm_ref, sm_ref, z_ref = refs[-8:]
    q_scale = {0: GLA_DK ** -0.5, 1: 1.0, 6: 1.0, 7: 1.0, 8: ML_DH ** -0.5, 9: ML_DH ** -0.5}
    for r0 in range(0, tm, SUB):
        rows = slice(r0, r0 + SUB)
        u = _norm_mod(_stream_rows(h_refs, rows, n_lat_tiles), mult_ref[0], shift_ref[0]).astype(BF16)
        for j in range(SMALL_COL // 256):
            z = _dot(u, w_ref[:, j * 256:(j + 1) * 256])
            if j in q_scale:
                rope = _rope_gla if j < 2 else _rope_ml
                c_ref, s_ref = (cg_ref, sg_ref) if j < 2 else (cm_ref, sm_ref)
                for half in range(2):
                    zz = rope(z[:, half * 128:(half + 1) * 128], c_ref[rows, :], s_ref[rows, :]) * q_scale[j]
                    z_ref[rows, j * 256 + half * 128:j * 256 + (half + 1) * 128] = zz
            else:
                z_ref[rows, j * 256:(j + 1) * 256] = z
        z_ref[rows, SMALL_COL:EVEN_N] = _dot(u, w_ref[:, SMALL_COL:EVEN_N])


def _even_proj(H, mult, shift, w, ropes, n_lat_rows, T, ctx_row):
    tm = SUB if isinstance(H, tuple) else TM
    n_lat_tiles, tiles_per_seq = n_lat_rows // tm, T // tm
    h_specs, h_args, R = _stream_operand(H, n_lat_tiles, tm)
    rowmap = _mod_row_map(n_lat_tiles, tiles_per_seq, ctx_row)
    ropemap = lambda i: (jnp.where(i < n_lat_tiles, i % tiles_per_seq, tiles_per_seq), 0)
    return pl.pallas_call(
        functools.partial(_even_proj_kernel, n_lat_tiles=n_lat_tiles, tm=tm),
        grid=(R // tm,),
        in_specs=h_specs + [
            pl.BlockSpec((1, 1, D), rowmap),
            pl.BlockSpec((1, 1, D), rowmap),
            _resident((D, EVEN_N)),
        ] + [pl.BlockSpec((tm, 128), ropemap)] * 4,
        out_specs=pl.BlockSpec((tm, EVEN_N), lambda i: (i, 0)),
        out_shape=jax.ShapeDtypeStruct((R, EVEN_N), F32),
        compiler_params=_params(),
        name="even_in_proj",
    )(*h_args, mult, shift, w, *ropes)


def _chunk_cummax(x, z, tmod):
    n = x.shape[0]
    k = 1
    while k < CHUNK:
        if z == 0:
            shifted, valid = pltpu.roll(x, k, 0), tmod >= k
        else:
            shifted, valid = pltpu.roll(x, n - k, 0), tmod < CHUNK - k
        x = jnp.maximum(x, jnp.where(valid, shifted, -jnp.inf))
        k *= 2
    return x


def _scan_kernel(*args):
    in_refs = (args[0:7], args[7:14])
    aup_ref, abias_ref, mgb_ref, of_ref, ob_ref, s_ref, c_ref, m_ref = args[14:]
    out_refs = (of_ref, ob_ref)
    dirs = (0, 1)
    n_chunks = SB // CHUNK
    order = (tuple(range(n_chunks)), tuple(reversed(range(n_chunks))))
    edge = lambda z, c: c * CHUNK + (CHUNK - 1 if z == 0 else 0)
    rows_of = lambda c: slice(c * CHUNK, (c + 1) * CHUNK)

    @pl.when(pl.program_id(1) == 0)
    def _():
        s_ref[...] = jnp.zeros_like(s_ref)
        c_ref[...] = jnp.zeros_like(c_ref)
        m_ref[...] = jnp.zeros_like(m_ref)

    ti = lax.broadcasted_iota(jnp.int32, (SB, SB), 0)
    si = lax.broadcasted_iota(jnp.int32, (SB, SB), 1)
    same = (ti // CHUNK) == (si // CHUNK)
    mask = [same & (si <= ti), same & (si >= ti)]
    tri = [x.astype(BF16) for x in mask]

    def chunk_totals(run, z):
        return jnp.concatenate([jnp.broadcast_to(run[edge(z, c):edge(z, c) + 1, :], (CHUNK, run.shape[1]))
                                for c in range(n_chunks)], axis=0)

    tmod = lax.broadcasted_iota(jnp.int32, (SB, 128), 0) % CHUNK
    lane256 = lax.broadcasted_iota(jnp.int32, (SB, GLA_H * GLA_DK), 1)
    in_head = [(lane256 // GLA_DK) == h for h in range(GLA_H)]
    lane128 = lax.broadcasted_iota(jnp.int32, (1, 128), 1)
    gate_col = lambda z, h: 2 * GLA_RANK + z * 2 * ML_H + h
    ones_b = jnp.ones((SB, ML_DH), BF16)

    P = []
    for z in dirs:
        gq_ref, gk_ref, gv_ref, sm_ref, mq_ref, mk_ref, mv_ref = in_refs[z]
        small = sm_ref[...]
        a_pre = _dot(small.astype(BF16), aup_ref[z]) + abias_ref[z]
        la_hi, la_lo = _split_bf16(_log_sigmoid(a_pre) * (1.0 / GLA_TAU))
        b = _dot(tri[z], la_hi) + _dot(tri[z], la_lo)
        gb = chunk_totals(b, z)
        q = gq_ref[...]
        k = gk_ref[...]
        gates = small + mgb_ref[...]
        lf_hi, lf_lo = _split_bf16(_log_sigmoid(gates))
        bc = pltpu.roll(_dot(tri[z], lf_hi) + _dot(tri[z], lf_lo), 128 - ML_H, 1)
        gc = chunk_totals(bc, z)
        w = gates - bc
        cm = _chunk_cummax(w, z, tmod)
        P.append(dict(
            q_dec=(q * jnp.exp(b)).astype(BF16), k_dec=(k * jnp.exp(-b)).astype(BF16),
            k_end=(k * jnp.exp(gb - b)).astype(BF16), gb=gb, bc=bc, gc=gc, w=w, cm=cm, w_t=w.T))

    G, M = {}, {}
    for z in dirs:
        gq_ref, gk_ref, gv_ref, sm_ref, mq_ref, mk_ref, mv_ref = in_refs[z]
        p = P[z]
        for h in range(GLA_H):
            qm = jnp.where(in_head[h], p["q_dec"], jnp.zeros_like(p["q_dec"]))
            km = jnp.where(in_head[h], p["k_end"], jnp.zeros_like(p["k_end"]))
            v = gv_ref[:, h * GLA_DV:(h + 1) * GLA_DV].astype(BF16)
            G[z, h] = dict(qm=qm, km=km, v=v, sc=_dot_nt(qm, p["k_dec"]))
        for h in range(GLA_H):
            e = G[z, h]
            e["intra"] = _dot(jnp.where(mask[z], e.pop("sc"), 0.0).astype(BF16), e["v"])
        for h in range(ML_H):
            fi = gate_col(z, h)
            cols = slice(h * ML_DH, (h + 1) * ML_DH)
            d = jnp.exp(jnp.where(mask[z], p["w_t"][fi:fi + 1, :] - p["cm"][:, fi:fi + 1], -jnp.inf))
            qb = mq_ref[:, cols].astype(BF16)
            va = jnp.concatenate([mv_ref[:, cols].astype(BF16), ones_b], axis=1)
            M[z, h] = dict(qb=qb, va=va, d=d, qk=_dot_nt(qb, mk_ref[:, cols].astype(BF16)))
        for h in range(ML_H):
            e = M[z, h]
            e["intra"] = _dot((e.pop("qk") * e.pop("d")).astype(BF16), e["va"])

    S = {(z, h): s_ref[z, h] for z in dirs for h in range(GLA_H)}
    C = {(z, h): c_ref[z, h] for z in dirs for h in range(ML_H)}
    m_old = [m_ref[z] for z in dirs]
    head_lanes = [functools.reduce(jnp.logical_or, [lane128 == gate_col(z, h) for h in range(ML_H)]) for z in dirs]
    for step in range(n_chunks):
        g_inter, m_inter = {}, {}
        for z in dirs:
            r = rows_of(order[z][step])
            for h in range(GLA_H):
                g_inter[z, h] = _dot_nt(G[z, h]["qm"][r], S[z, h].astype(BF16))
            for h in range(ML_H):
                m_inter[z, h] = _dot(M[z, h]["qb"][r], C[z, h].astype(BF16))
            c = order[z][step]
            p = P[z]
            wmax = p["cm"][edge(z, c):edge(z, c) + 1, :]
            w_end = jnp.exp(p["w"][r] - wmax)
            p["m_loc", c] = wmax + p["gc"][c * CHUNK:c * CHUNK + 1, :]
            for h in range(GLA_H):
                e = G[z, h]
                e["kv", c] = _dot_tn(e["v"][r], e["km"][r])
            for h in range(ML_H):
                e = M[z, h]
                fi = gate_col(z, h)
                kw = (in_refs[z][5][r, h * ML_DH:(h + 1) * ML_DH] * w_end[:, fi:fi + 1]).astype(BF16)
                e["cl", c] = _dot_tn(kw, e["va"][r])
        for z in dirs:
            c = order[z][step]
            r = rows_of(c)
            p = P[z]
            decay = jnp.exp(p["gb"][c * CHUNK:c * CHUNK + 1, :])
            for h in range(GLA_H):
                o = G[z, h]["intra"][r] + g_inter[z, h]
                out_refs[z][r, h * GLA_DV:(h + 1) * GLA_DV] = o.astype(out_refs[z].dtype)
                S[z, h] = decay * S[z, h] + G[z, h]["kv", c]
            mx = jnp.maximum(m_old[z], p["cm"][r])
            alpha = jnp.exp(p["cm"][r] - mx)
            inter_w = jnp.exp(m_old[z] - mx)
            floor = jnp.exp(-(p["bc"][r] + mx))
            g_c = p["gc"][c * CHUNK:c * CHUNK + 1, :]
            m_new = jnp.where(head_lanes[z], jnp.maximum(g_c + m_old[z], p["m_loc", c]), 0.0)
            a_sc = jnp.exp(g_c + m_old[z] - m_new)
            w_sc = jnp.exp(p["m_loc", c] - m_new)
            for h in range(ML_H):
                fi = gate_col(z, h)
                al, iw = alpha[:, fi:fi + 1], inter_w[:, fi:fi + 1]
                tot = al * M[z, h]["intra"][r] + iw * m_inter[z, h]
                den = jnp.maximum(jnp.abs(tot[:, ML_DH:]), floor[:, fi:fi + 1])
                col = GLA_H * GLA_DV + h * ML_DH
                out_refs[z][r, col:col + ML_DH] = (tot[:, :ML_DH] / den).astype(out_refs[z].dtype)
                C[z, h] = a_sc[:, fi:fi + 1] * C[z, h] + w_sc[:, fi:fi + 1] * M[z, h]["cl", c]
            m_old[z] = m_new
    for z in dirs:
        for h in range(GLA_H):
            s_ref[z, h] = S[z, h]
        for h in range(ML_H):
            c_ref[z, h] = C[z, h]
        m_ref[z] = m_old[z]


def _scan(Z, aup, abias, mgb, B, T, LC):
    R = Z.shape[0]
    assert LC == SB and T % SB == 0
    nb = T // SB
    ctx0 = B * nb
    fmap = lambda b, s: jnp.where(s == 0, ctx0 + b, b * nb + s - 1)
    bmap = lambda b, s: jnp.where(s == 0, ctx0 + b, b * nb + nb - s)

    def specs(rowmap):
        col = lambda width, blk: pl.BlockSpec((SB, width), lambda b, s: (rowmap(b, s), blk))
        return [col(256, 0), col(256, 1), col(512, 1), col(128, SMALL_COL // 128),
                col(512, 3), col(512, 4), col(512, 5)]

    out_spec = lambda rowmap: pl.BlockSpec((SB, D), lambda b, s: (rowmap(b, s), 0))
    return pl.pallas_call(
        _scan_kernel,
        grid=(B, nb + 1),
        in_specs=specs(fmap) + specs(bmap) + [
            pl.BlockSpec((2, 128, 256), lambda b, s: (0, 0, 0)),
            pl.BlockSpec((2, 1, 256), lambda b, s: (0, 0, 0)),
            pl.BlockSpec((1, 128), lambda b, s: (0, 0)),
        ],
        out_specs=[out_spec(fmap), out_spec(bmap)],
        out_shape=[jax.ShapeDtypeStruct((R, D), BF16)] * 2,
        scratch_shapes=[
            pltpu.VMEM((2, GLA_H, GLA_DV, GLA_H * GLA_DK), F32),
            pltpu.VMEM((2, ML_H, ML_DH, 2 * ML_DH), F32),
            pltpu.VMEM((2, 1, 128), F32),
        ],
        compiler_params=_params(2),
        name="gla_mlstm_scan",
    )(*([Z] * 14), aup, abias, mgb)


def _even_out_kernel(*refs, n_lat_tiles):
    of_ref, ob_ref, gg_ref, mo_ref = refs[:4]
    h_refs = refs[4:-6]
    gate_ref, gnw_ref, mnw_ref, w_ref, o_ref, y_ref = refs[-6:]
    for r0 in range(0, TM, SUB):
        rows = slice(r0, r0 + SUB)
        for hh in range(GLA_H + ML_H):
            cols = slice(hh * 128, (hh + 1) * 128)
            x = of_ref[rows, cols].astype(F32) + ob_ref[rows, cols].astype(F32)
            ms = jnp.mean(x * x, axis=-1, keepdims=True)
            if hh < GLA_H:
                gt = gg_ref[rows, cols]
                y = x * lax.rsqrt(ms + EPS) * gnw_ref[...] * (gt * _sigmoid(gt))
            else:
                gt = mo_ref[rows, (hh - GLA_H) * 128:(hh - GLA_H + 1) * 128]
                y = x * lax.rsqrt(ms + EPS) * mnw_ref[...] * _sigmoid(gt)
            y_ref[rows, cols] = y.astype(BF16)
        o_ref[rows, :] = _stream_rows(h_refs, rows, n_lat_tiles) + gate_ref[0] * _dot(y_ref[rows, :], w_ref[...])


def _even_out(of, ob, Z, H, gate, gnw, mnw, w, n_lat_tiles, tiles_per_seq, ctx_row):
    h_specs, h_args, R = _stream_operand(H, n_lat_tiles)
    rowmap = _mod_row_map(n_lat_tiles, tiles_per_seq, ctx_row)
    return pl.pallas_call(
        functools.partial(_even_out_kernel, n_lat_tiles=n_lat_tiles),
        grid=(R // TM,),
        in_specs=[
            pl.BlockSpec((TM, D), lambda i: (i, 0)),
            pl.BlockSpec((TM, D), lambda i: (i, 0)),
            pl.BlockSpec((TM, 512), lambda i: (i, 2)),
            pl.BlockSpec((TM, 512), lambda i: (i, 6)),
        ] + h_specs + [
            pl.BlockSpec((1, 1, D), rowmap),
            pl.BlockSpec((1, 128), lambda i: (0, 0)),
            pl.BlockSpec((1, 128), lambda i: (0, 0)),
            _resident((D, D)),
        ],
        out_specs=pl.BlockSpec((TM, D), lambda i: (i, 0)),
        out_shape=jax.ShapeDtypeStruct((R, D), F32),
        scratch_shapes=[pltpu.VMEM((TM, D), BF16)],
        input_output_aliases={4: 0} if len(h_args) == 1 else {},
        compiler_params=_params(),
        name="even_out_proj",
    )(of, ob, Z, Z, *h_args, gate, gnw, mnw, w)


def _na_proj_kernel(h_ref, mult_ref, shift_ref, w_ref, z_ref):
    for r0 in range(0, TM, SUB):
        rows = slice(r0, r0 + SUB)
        u = _norm_mod(h_ref[rows, :], mult_ref[0], shift_ref[0]).astype(BF16)
        for j in range(3 * D // 256):
            z = _dot(u, w_ref[:, j * 256:(j + 1) * 256])
            if j < D // 256:
                z = z * (NA_DH ** -0.5)
            z_ref[rows, j * 256:(j + 1) * 256] = z.astype(BF16)


def _na_proj(H, mult, shift, w, n_lat_tiles, tiles_per_seq, ctx_row):
    R = H.shape[0]
    rowmap = _mod_row_map(n_lat_tiles, tiles_per_seq, ctx_row)
    return pl.pallas_call(
        _na_proj_kernel,
        grid=(R // TM,),
        in_specs=[
            pl.BlockSpec((TM, D), lambda i: (i, 0)),
            pl.BlockSpec((1, 1, D), rowmap),
            pl.BlockSpec((1, 1, D), rowmap),
            _resident((D, 3 * D)),
        ],
        out_specs=pl.BlockSpec((TM, 3 * D), lambda i: (i, 0)),
        out_shape=jax.ShapeDtypeStruct((R, 3 * D), BF16),
        compiler_params=_params(),
        name="na_qkv_proj",
    )(H, mult, shift, w)


NA_ROWS_PER_STEP = 4


def _stack_heads(q):
    lane = lax.broadcasted_iota(jnp.int32, q.shape, 1)
    zero = jnp.zeros_like(q)
    return jnp.concatenate([jnp.where(lane < NA_DH, q, zero), jnp.where(lane >= NA_DH, q, zero)], axis=0)


def _softmax_pv(s_list, v_list):
    m = functools.reduce(jnp.maximum, [jnp.max(s, axis=1, keepdims=True) for s in s_list])
    p_list = [jnp.exp(s - m) for s in s_list]
    l = functools.reduce(jnp.add, [jnp.sum(p, axis=1, keepdims=True) for p in p_list])
    o = functools.reduce(jnp.add, [_dot(p.astype(BF16), v) for p, v in zip(p_list, v_list)])
    return o / l


def _unstack_heads(o):
    n = o.shape[0] // 2
    lane = lax.broadcasted_iota(jnp.int32, (n, 128), 1)
    return jnp.where(lane < NA_DH, o[:n], o[n:])


def _na_kernel(q_ref, k_ref, v_ref, kc_ref, vc_ref, qc_ref, tb_ref, o_ref, oc_ref, sw_ref, sc_ref, *, rows):
    kc = kc_ref[...]
    vc = vc_ref[...]
    win = NA_WIN_H * GRID_W
    n2 = 2 * GRID_W
    rps = NA_ROWS_PER_STEP
    n_groups = rows // rps

    def group_rows(g):
        rs = [g * rps + u for u in range(rps)]
        kr0 = [jnp.clip(r - NA_WIN_H // 2, 0, rows - NA_WIN_H) for r in rs]
        qrows = [pl.ds(pl.multiple_of(r * GRID_W, GRID_W), GRID_W) for r in rs]
        krows = [pl.ds(pl.multiple_of(k0 * GRID_W, GRID_W), win) for k0 in kr0]
        return rs, kr0, qrows, krows

    def scores(g, slot):
        rs, kr0, qrows, krows = group_rows(g)
        q2 = [_stack_heads(q_ref[qr, :]) for qr in qrows]
        s_w = [_dot_nt(q, k_ref[kr, :]) for q, kr in zip(q2, krows)]
        sc_ref[slot] = _dot_nt(jnp.concatenate(q2, axis=0), kc)
        for u, (s, r, k0) in enumerate(zip(s_w, rs, kr0)):
            bias = [tb_ref[0, j + NA_WIN_H - 1 - (r - k0)] for j in range(0, NA_WIN_H, 2)]
            sw_ref[slot, u] = s + jnp.concatenate(bias, axis=1)

    def softmax_pv(g, slot):
        _, _, qrows, krows = group_rows(g)
        s_w = [sw_ref[slot, u] for u in range(rps)]
        s_c = [sc_ref[slot, u * n2:(u + 1) * n2, :] for u in range(rps)]
        m = [jnp.maximum(jnp.max(a, axis=1, keepdims=True), jnp.max(b, axis=1, keepdims=True))
             for a, b in zip(s_w, s_c)]
        p_w = [jnp.exp(s - mm) for s, mm in zip(s_w, m)]
        p_c = [jnp.exp(s - mm) for s, mm in zip(s_c, m)]
        l = [jnp.sum(a, axis=1, keepdims=True) + jnp.sum(b, axis=1, keepdims=True) for a, b in zip(p_w, p_c)]
        o_c_all = _dot(jnp.concatenate([b.astype(BF16) for b in p_c], axis=0), vc)
        o = [_dot(a.astype(BF16), v_ref[kr, :]) + o_c_all[u * n2:(u + 1) * n2]
             for u, (a, kr) in enumerate(zip(p_w, krows))]
        for oo, ll, qr in zip(o, l, qrows):
            o_ref[qr, :] = _unstack_heads(oo / ll).astype(o_ref.dtype)

    scores(0, 0)

    def pair_body(i, carry):
        g = 2 * i
        scores(g + 1, 1)
        softmax_pv(g, 0)
        scores(jnp.minimum(g + 2, n_groups - 1), 0)
        softmax_pv(g + 1, 1)
        return carry

    lax.fori_loop(0, n_groups // 2, pair_body, 0)

    oc = _softmax_pv([_dot_nt(_stack_heads(qc_ref[...]), kc)], [vc])
    oc_ref[...] = _unstack_heads(oc).astype(oc_ref.dtype)


def _na_attention(Zq, tb, B, T, LC):
    rows = T // GRID_W
    assert rows % (2 * NA_ROWS_PER_STEP) == 0 and rows >= NA_WIN_H
    n_pairs = NA_H // 2
    score_slots = [pltpu.VMEM((2, NA_ROWS_PER_STEP, 2 * GRID_W, NA_WIN_H * GRID_W), F32),
                   pltpu.VMEM((2, NA_ROWS_PER_STEP * 2 * GRID_W, LC), F32)]
    ctx_blk0 = B * T // LC
    lat = lambda part: pl.BlockSpec((T, 128), lambda b, p: (b, part * n_pairs + p))
    ctx = lambda part: pl.BlockSpec((LC, 128), lambda b, p: (ctx_blk0 + b, part * n_pairs + p))
    return pl.pallas_call(
        functools.partial(_na_kernel, rows=rows),
        grid=(B, n_pairs),
        in_specs=[lat(0), lat(1), lat(2), ctx(1), ctx(2), ctx(0),
                  pl.BlockSpec((1, 2 * NA_WIN_H - 2, 2 * GRID_W, 2 * GRID_W), lambda b, p: (p, 0, 0, 0))],
        out_specs=[pl.BlockSpec((T, 128), lambda b, p: (b, p)),
                   pl.BlockSpec((LC, 128), lambda b, p: (b, p))],
        out_shape=[jax.ShapeDtypeStruct((B * T, D), BF16), jax.ShapeDtypeStruct((B * LC, D), BF16)],
        scratch_shapes=score_slots,
        compiler_params=_params(2),
        name="na_attention",
    )(Zq, Zq, Zq, Zq, Zq, Zq, tb)


def _na_bias_tables(rpb):
    c, kc = np.meshgrid(np.arange(GRID_W), np.arange(GRID_W), indexing="ij")
    start = np.clip(c - NA_WIN_W // 2, 0, GRID_W - NA_WIN_W)
    inside = (kc >= start) & (kc < start + NA_WIN_W)
    onehot = np.zeros((2 * NA_WIN_W - 1, GRID_W, GRID_W), np.float32)
    onehot[(kc - c + NA_WIN_W - 1)[inside], c[inside], kc[inside]] = 1.0
    toep = jnp.einsum("hdi,ick->hdck", rpb.astype(F32), onehot, precision=lax.Precision.HIGHEST)
    toep = jnp.where(inside[None, None], toep, -1e30)
    two = jnp.concatenate([toep[:, :-1], toep[:, 1:]], axis=-1)
    n_d = 2 * NA_WIN_H - 2
    return (two.reshape(NA_H // 2, 2, n_d, GRID_W, 2 * GRID_W).transpose(0, 2, 1, 3, 4)
            .reshape(NA_H // 2, n_d, 2 * GRID_W, 2 * GRID_W))


def _na_out_kernel(yl_ref, yc_ref, h_ref, gate_ref, w_ref, o_ref, *, n_lat_tiles):
    y = jnp.where(pl.program_id(0) < n_lat_tiles, yl_ref[...], yc_ref[...])
    o_ref[...] = h_ref[...] + gate_ref[0] * _dot(y, w_ref[...])


def _na_out(Y_lat, Y_ctx, H, gate, w, n_lat_tiles, tiles_per_seq, ctx_row):
    R = H.shape[0]
    rowmap = _mod_row_map(n_lat_tiles, tiles_per_seq, ctx_row)
    return pl.pallas_call(
        functools.partial(_na_out_kernel, n_lat_tiles=n_lat_tiles),
        grid=(R // TM,),
        in_specs=[
            pl.BlockSpec((TM, D), lambda i: (jnp.minimum(i, n_lat_tiles - 1), 0)),
            pl.BlockSpec((TM, D), lambda i: (jnp.maximum(i - n_lat_tiles, 0), 0)),
            pl.BlockSpec((TM, D), lambda i: (i, 0)),
            pl.BlockSpec((1, 1, D), rowmap),
            _resident((D, D)),
        ],
        out_specs=pl.BlockSpec((TM, D), lambda i: (i, 0)),
        out_shape=jax.ShapeDtypeStruct((R, D), F32),
        input_output_aliases={2: 0},
        compiler_params=_params(),
        name="na_out_proj",
    )(Y_lat, Y_ctx, H, gate, w)


def _ffn_kernel(hp_ref, h_ref, hn_ref, mult_ref, shift_ref, gate_ref, wup_ref, cw_ref, cb_ref, wd_ref, fnw_ref,
                o_ref, u_ref, act_ref, *, final, n_lat_rows, T, LC):
    x = h_ref[...]
    mult = mult_ref[0]
    shift = shift_ref[0]
    u_ref[0:HALO, :] = _norm_mod(hp_ref[...], mult, shift).astype(BF16)
    u_ref[HALO:HALO + TMF, :] = _norm_mod(x, mult, shift).astype(BF16)
    u_ref[HALO + TMF:, :] = _norm_mod(hn_ref[...], mult, shift).astype(BF16)

    r = pl.program_id(0) * TMF + lax.broadcasted_iota(jnp.int32, (TMF, 1), 0)
    is_lat = r < n_lat_rows
    pos = jnp.where(is_lat, r & (T - 1), r & (LC - 1))
    first = pos == 0
    last = pos == jnp.where(is_lat, T - 1, LC - 1)

    n_ext = TMF + 2 * HALO
    for c in range(D_FF // FC):
        cols = slice(c * FC, (c + 1) * FC)
        a = _dot(u_ref[...], wup_ref[:, cols])
        a_prev = pltpu.roll(a, 1, 0)[HALO:HALO + TMF]
        a_next = pltpu.roll(a, n_ext - 1, 0)[HALO:HALO + TMF]
        a_mid = a[HALO:HALO + TMF]
        conv = (cw_ref[0:1, cols] * jnp.where(first, 0.0, a_prev) + cw_ref[1:2, cols] * a_mid
                + cw_ref[2:3, cols] * jnp.where(last, 0.0, a_next) + cb_ref[:, cols])
        gt = _dot(u_ref[HALO:HALO + TMF, :], wup_ref[:, D_FF + c * FC:D_FF + (c + 1) * FC])
        act_ref[:, cols] = (jax.nn.gelu(conv, approximate=True) * gt).astype(BF16)

    y = x + gate_ref[0] * _dot(act_ref[...], wd_ref[...])
    if final:
        y = _norm_mod(y, fnw_ref[...], 0.0)
    o_ref[...] = y


def _ffn(H, mult, shift, gate, wup, cw, cb, wd, fnw, *, final, n_rows_out, ctx_row, n_lat_rows, T, LC):
    R = H.shape[0]
    rowmap = _mod_row_map(n_lat_rows // TMF, T // TMF, ctx_row)
    per = TMF // HALO
    last_halo_blk = R // HALO - 1
    kern = functools.partial(_ffn_kernel, final=final, n_lat_rows=n_lat_rows, T=T, LC=LC)
    return pl.pallas_call(
        kern,
        grid=(n_rows_out // TMF,),
        in_specs=[
            pl.BlockSpec((HALO, D), lambda i: (jnp.maximum(i * per - 1, 0), 0)),
            pl.BlockSpec((TMF, D), lambda i: (i, 0)),
            pl.BlockSpec((HALO, D), lambda i: (jnp.minimum((i + 1) * per, last_halo_blk), 0)),
            pl.BlockSpec((1, 1, D), rowmap),
            pl.BlockSpec((1, 1, D), rowmap),
            pl.BlockSpec((1, 1, D), rowmap),
            _resident((D, 2 * D_FF)),
            pl.BlockSpec((3, D_FF), lambda i: (0, 0)),
            pl.BlockSpec((1, D_FF), lambda i: (0, 0)),
            _resident((D_FF, D)),
            pl.BlockSpec((1, D), lambda i: (0, 0)),
        ],
        out_specs=pl.BlockSpec((TMF, D), lambda i: (i, 0)),
        out_shape=jax.ShapeDtypeStruct((n_rows_out, D), F32),
        scratch_shapes=[pltpu.VMEM((TMF + 2 * HALO, D), BF16), pltpu.VMEM((TMF, D_FF), BF16)],
        compiler_params=_params(),
        name="conv_ffn_final" if final else "conv_ffn",
    )(H, H, H, mult, shift, gate, wup, cw, cb, wd, fnw)


def _rope_tables(T, head_dim):
    pos = jnp.arange(T, dtype=jnp.int32)
    row = (pos // GRID_W).astype(F32)
    col = (pos % GRID_W).astype(F32)
    n_freq = head_dim // 4
    inv_freq = ROPE_BASE ** (-jnp.arange(n_freq, dtype=F32) / n_freq)
    ang = jnp.concatenate([row[:, None] * inv_freq, col[:, None] * inv_freq], axis=-1)
    cos, sin = jnp.cos(ang), jnp.sin(ang)
    reps = 128 // head_dim
    c = jnp.tile(jnp.concatenate([cos, cos], axis=-1), (1, reps))
    s = jnp.tile(jnp.concatenate([-sin, sin], axis=-1), (1, reps))
    c = jnp.concatenate([c, jnp.ones((TM, 128), F32)], axis=0)
    s = jnp.concatenate([s, jnp.zeros((TM, 128), F32)], axis=0)
    return c, s


def kernel(x, c, ctx, c_ctx, mod_w, mod_b, norm_mix_w, norm_ffn_w, ev_w_in, ev_w_out, gla_a_up, gla_a_bias,
           gla_norm_w, ml_gate_bias, ml_norm_w, na_w_qkv, na_w_out, na_rpb, ffn_w_up, ffn_conv_w, ffn_conv_b,
           ffn_w_down, final_norm_w):
    B, T, _ = x.shape
    LC = ctx.shape[1]
    assert x.shape[2] == D and T % TM == 0 and (B * LC) % TM == 0 and B < N_MOD_ROWS
    assert T & (T - 1) == 0 and LC & (LC - 1) == 0 and T % GRID_W == 0
    n_lat_rows = B * T
    n_lat_tiles = n_lat_rows // TM
    tiles_per_seq = T // TM
    tile_kw = dict(n_lat_tiles=n_lat_tiles, tiles_per_seq=tiles_per_seq, ctx_row=B)

    H = (x.reshape(B * T, D), ctx.reshape(B * LC, D))
    R = n_lat_rows + B * LC

    cvec = jnp.zeros((N_MOD_ROWS, D), F32).at[:B].set(c).at[B].set(c_ctx)
    mods = _modulation(cvec, mod_w, mod_b)
    ropes = _rope_tables(T, GLA_DK) + _rope_tables(T, ML_DH)

    out = None
    for layer in range(DEPTH):
        j = layer // 2
        last = layer == DEPTH - 1
        sh1, sc1, g1, sh2, sc2, g2 = [m.reshape(N_MOD_ROWS, 1, D) for m in jnp.split(mods[layer], 6, axis=-1)]
        mult1 = norm_mix_w[layer] * (1.0 + sc1)
        mult2 = norm_ffn_w[layer] * (1.0 + sc2)
        if layer % 2 == 0:
            w = ev_w_in[j]
            w_re = jnp.concatenate(
                [w[:, 0:1536], w[:, 1568:3616], w[:, 1536:1568], w[:, 3616:3632],
                 jnp.zeros((D, EVEN_N - 3632), F32)], axis=1).astype(BF16)
            Z = _even_proj(H, mult1, sh1, w_re, ropes, n_lat_rows, T, B)
            aup = jnp.zeros((2, 128, GLA_H * GLA_DK), F32)
            aup = aup.at[0, 0:GLA_RANK].set(gla_a_up[j, 0]).at[1, GLA_RANK:2 * GLA_RANK].set(gla_a_up[j, 1])
            mgb = jnp.zeros((1, 128), F32).at[0, 2 * GLA_RANK:2 * GLA_RANK + 4 * ML_H].set(ml_gate_bias[j].reshape(-1))
            of, ob = _scan(Z, aup.astype(BF16), gla_a_bias[j].reshape(2, 1, -1), mgb, B, T, LC)
            H = _even_out(of, ob, Z, H, g1, gla_norm_w[j].reshape(1, -1), ml_norm_w[j].reshape(1, -1),
                          ev_w_out[j].astype(BF16), **tile_kw)
        else:
            Zq = _na_proj(H, mult1, sh1, na_w_qkv[j].astype(BF16), **tile_kw)
            o_lat, o_ctx = _na_attention(Zq, _na_bias_tables(na_rpb[j]), B, T, LC)
            H = _na_out(o_lat, o_ctx, H, g1, na_w_out[j].astype(BF16), **tile_kw)
        res = _ffn(H, mult2, sh2, g2, ffn_w_up[layer].astype(BF16), ffn_conv_w[layer],
                   ffn_conv_b[layer].reshape(1, -1), ffn_w_down[layer].astype(BF16), final_norm_w.reshape(1, -1),
                   final=last, n_rows_out=n_lat_rows if last else R, ctx_row=B, n_lat_rows=n_lat_rows, T=T, LC=LC)
        if last:
            out = res
        else:
            H = res
    return out.reshape(B, T, D)
```
